```python
import math
import jax, jax.numpy as jnp
from jax import lax
import numpy as np

D_MODEL = 1024
BATCH = 16
SEQ = 2048
DEPTH = 1

DN_HEADS = 4
DN_HEAD_DIM = 128
DN_WIDTH = DN_HEADS * DN_HEAD_DIM
DN_CONV = 4
DN_CHUNK = 64
POOL_WINDOWS = (2, 4, 8, 16)
POOL_GROUPS = len(POOL_WINDOWS)
POOL_WIDTH = D_MODEL // 2
POOL_GROUP_DIM = POOL_WIDTH // POOL_GROUPS
N_BRANCHES = 2
IN_COLS = 4 * DN_WIDTH + 2 * DN_HEADS + POOL_WIDTH + N_BRANCHES * D_MODEL
MOE_GROUPS = 4
MOE_EXPERTS_PER_GROUP = 8
MOE_EXPERTS = MOE_GROUPS * MOE_EXPERTS_PER_GROUP
MOE_TOP_K = 2
MOE_D_FF = D_MODEL // 4
MOE_BLOCK = 128
N_MOD = 6
EPS = 1e-6

kernel_name = "hybrid_deltanet_pool_hmoe_adaln"


def _rmsnorm(x, g):
    x32 = x.astype(jnp.float32)
    y = x32 * lax.rsqrt(jnp.mean(x32 * x32, axis=-1, keepdims=True) + EPS)
    return (y * g.astype(jnp.float32)).astype(x.dtype)


def _l2norm(x):
    return x * lax.rsqrt(jnp.sum(x * x, axis=-1, keepdims=True) + EPS)


def _split_in(p):
    sizes = [3 * DN_WIDTH, DN_WIDTH, DN_HEADS, DN_HEADS, POOL_WIDTH, D_MODEL, D_MODEL]
    idx = [int(v) for v in np.cumsum(sizes)[:-1]]
    return jnp.split(p, idx, axis=-1)


def _causal_dwconv(x, w):
    K, C = w.shape
    return lax.conv_general_dilated(
        x, w[:, None, :].astype(x.dtype), window_strides=(1,), padding=[(K - 1, 0)],
        dimension_numbers=('NWC', 'WIO', 'NWC'), feature_group_count=C)


def _gated_delta_rule(q, k, v, g, beta):
    B, S, H, Dk = q.shape
    Dv = v.shape[-1]
    C = DN_CHUNK
    N = S // C

    def chunks(t):
        return jnp.moveaxis(t.reshape(B, N, C, H, -1), 3, 1)

    q, k, v = chunks(q), chunks(k), chunks(v)
    g = chunks(g[..., None])[..., 0]
    beta = chunks(beta[..., None])[..., 0]
    gc = jnp.cumsum(g, axis=-1)
    causal = jnp.tril(jnp.ones((C, C), bool))
    strict = jnp.tril(jnp.ones((C, C), bool), -1)
    decay = jnp.exp(jnp.where(causal, gc[..., :, None] - gc[..., None, :], -jnp.inf))
    k_beta = k * beta[..., None]
    v_beta = v * beta[..., None]
    Lm = jnp.where(strict, jnp.einsum('bhnid,bhnjd->bhnij', k_beta, k) * decay, 0.0)
    eye = jnp.eye(C, dtype=jnp.float32)
    T = lax.linalg.triangular_solve(Lm + eye, jnp.broadcast_to(eye, Lm.shape),
                                    left_side=True, lower=True, unit_diagonal=True)
    u = jnp.einsum('bhnij,bhnjv->bhniv', T, v_beta)
    w = jnp.einsum('bhnij,bhnjk->bhnik', T, k_beta * jnp.exp(gc)[..., None])
    attn = jnp.einsum('bhnid,bhnjd->bhnij', q, k) * decay
    g_last = gc[..., -1]
    q_dec = q * jnp.exp(gc)[..., None]
    k_dec = k * jnp.exp(g_last[..., None] - gc)[..., None]
    xs = tuple(jnp.moveaxis(t, 2, 0) for t in (u, w, q_dec, k_dec, attn, jnp.exp(g_last)))

    def step(state, inp):
        u_i, w_i, qd_i, kd_i, a_i, dl_i = inp
        v_new = u_i - jnp.einsum('bhck,bhkv->bhcv', w_i, state)
        o_i = jnp.einsum('bhck,bhkv->bhcv', qd_i, state) + jnp.einsum('bhij,bhjv->bhiv', a_i, v_new)
        state = state * dl_i[..., None, None] + jnp.einsum('bhck,bhcv->bhkv', kd_i, v_new)
        return state, o_i

    state0 = jnp.zeros((B, H, Dk, Dv), jnp.float32)
    _, o = lax.scan(step, state0, xs)
    return jnp.transpose(o, (1, 0, 3, 2, 4)).reshape(B, S, H, Dv)


def _multiscale_pool(u, pool_w, pool_scale):
    B, S, _ = u.shape
    u32 = u.astype(jnp.float32)
    cs = jnp.pad(jnp.cumsum(u32, axis=1), ((0, 0), (1, 0), (0, 0)))
    t = jnp.arange(S)
    outs = []
    for gi, win in enumerate(POOL_WINDOWS):
        csg = cs[..., gi * POOL_GROUP_DIM:(gi + 1) * POOL_GROUP_DIM]
        lo = jnp.maximum(t + 1 - win, 0)
        cnt = jnp.minimum(t + 1, win).astype(jnp.float32)
        outs.append((csg[:, 1:] - csg[:, lo]) / cnt[None, :, None])
    pooled = jnp.stack(outs, axis=2) - u32.reshape(B, S, POOL_GROUPS, POOL_GROUP_DIM)
    y = jnp.einsum('bsgc,gcd->bsgd', pooled.astype(u.dtype), pool_w)
    return y.reshape(B, S, POOL_WIDTH) * pool_scale


def _token_mixer(xn, w_in, conv_w, a_log, dt_bias, dn_norm_g, pool_w, pool_scale, w_lift_a, w_lift_b, w_out):
    B, S, _ = xn.shape
    proj = xn @ w_in
    qkv, z, a, b, pu, ga, gb = _split_in(proj)
    qkv = jax.nn.silu(_causal_dwconv(qkv, conv_w)).astype(jnp.float32)
    q, k, v = jnp.split(qkv, 3, axis=-1)
    q = _l2norm(q.reshape(B, S, DN_HEADS, DN_HEAD_DIM)) * (DN_HEAD_DIM ** -0.5)
    k = _l2norm(k.reshape(B, S, DN_HEADS, DN_HEAD_DIM))
    v = v.reshape(B, S, DN_HEADS, DN_HEAD_DIM)
    beta = jax.nn.sigmoid(b.astype(jnp.float32))
    g = -jnp.exp(a_log.astype(jnp.float32)) * jax.nn.softplus(a.astype(jnp.float32) + dt_bias.astype(jnp.float32))
    o = _gated_delta_rule(q, k, v, g, beta)
    zg = jax.nn.silu(z.astype(jnp.float32)).reshape(B, S, DN_HEADS, DN_HEAD_DIM)
    o = o * lax.rsqrt(jnp.mean(o * o, axis=-1, keepdims=True) + EPS) * dn_norm_g.astype(jnp.float32) * zg
    y_a = o.reshape(B, S, DN_WIDTH).astype(xn.dtype)
    y_b = _multiscale_pool(pu, pool_w, pool_scale)
    mixed = jax.nn.sigmoid(ga) * (y_a @ w_lift_a) + jax.nn.sigmoid(gb) * (y_b @ w_lift_b)
    return mixed @ w_out


def _hier_moe(x, w_rg, b_rg, w_re, b_re, w_gate, w_up, w_down):
    B, S, D = x.shape
    N = B * S
    xf = x.reshape(N, D)
    pg = jax.nn.softmax((xf @ w_rg + b_rg).astype(jnp.float32), axis=-1)
    p_grp, grp = lax.top_k(pg, 1)
    le = (xf @ w_re + b_re).astype(jnp.float32).reshape(N, MOE_GROUPS, MOE_EXPERTS_PER_GROUP)
    le = jnp.take_along_axis(le, grp[:, :, None], axis=1)[:, 0]
    top_p, top_i = lax.top_k(jax.nn.softmax(le, axis=-1), MOE_TOP_K)
    wts = p_grp * top_p / jnp.sum(top_p, axis=-1, keepdims=True)
    eid = grp * MOE_EXPERTS_PER_GROUP + top_i
    A = N * MOE_TOP_K
    e_flat = eid.reshape(A)
    tok_flat = jnp.repeat(jnp.arange(N, dtype=jnp.int32), MOE_TOP_K)
    w_flat = wts.reshape(A)
    order = jnp.argsort(e_flat)
    e_s, tok_s, w_s = e_flat[order], tok_flat[order], w_flat[order]
    counts = jnp.bincount(e_flat, length=MOE_EXPERTS)
    padded = (counts + MOE_BLOCK - 1) // MOE_BLOCK * MOE_BLOCK
    start = jnp.cumsum(counts) - counts
    pend = jnp.cumsum(padded)
    pstart = pend - padded
    dest = pstart[e_s] + jnp.arange(A) - start[e_s]
    P = A + MOE_EXPERTS * MOE_BLOCK
    nb = P // MOE_BLOCK
    slot_tok = jnp.full((P,), N, jnp.int32).at[dest].set(tok_s)
    slot_w = jnp.zeros((P,), jnp.float32).at[dest].set(w_s)
    blk_e = jnp.minimum(jnp.searchsorted(pend, jnp.arange(nb) * MOE_BLOCK, side='right'), MOE_EXPERTS - 1)
    x_pad = jnp.concatenate([xf, jnp.zeros((1, D), xf.dtype)], axis=0)
    xb = x_pad[slot_tok].reshape(nb, MOE_BLOCK, D)

    def expert_block(args):
        xblk, e = args
        h = jax.nn.silu(xblk @ w_gate[e]) * (xblk @ w_up[e])
        return h @ w_down[e]

    yb = lax.map(expert_block, (xb, blk_e))
    y = jnp.zeros((N + 1, D), jnp.float32).at[slot_tok].add(yb.reshape(P, D).astype(jnp.float32) * slot_w[:, None])
    return y[:N].astype(x.dtype).reshape(B, S, D)


def setup_inputs(seed: int = 0) -> dict:
    key = jax.random.key(seed)
    ks = jax.random.split(key, 32)
    L, D = DEPTH, D_MODEL
    f32 = jnp.float32

    def nrm(k, shape, scale):
        return jax.random.normal(k, shape, f32) * scale

    dt = jnp.exp(jax.random.uniform(ks[8], (L, DN_HEADS), f32, math.log(1e-3), math.log(1e-1)))
    return {
        "x": nrm(ks[0], (BATCH, SEQ, D), 1.0),
        "c": nrm(ks[1], (BATCH, D), 1.0),
        "w_ada": nrm(ks[2], (L, D, N_MOD * D), 0.5 * D ** -0.5),
        "b_ada": nrm(ks[3], (L, N_MOD * D), 0.02),
        "norm1_g": 1.0 + nrm(ks[4], (L, D), 0.1),
        "w_in": nrm(ks[5], (L, D, IN_COLS), D ** -0.5),
        "conv_w": nrm(ks[6], (L, DN_CONV, 3 * DN_WIDTH), DN_CONV ** -0.5),
        "a_log": jnp.log(jax.random.uniform(ks[7], (L, DN_HEADS), f32, 1.0, 16.0)),
        "dt_bias": dt + jnp.log(-jnp.expm1(-dt)),
        "dn_norm_g": 1.0 + nrm(ks[9], (L, DN_HEAD_DIM), 0.1),
        "pool_w": nrm(ks[10], (L, POOL_GROUPS, POOL_GROUP_DIM, POOL_GROUP_DIM), POOL_GROUP_DIM ** -0.5),
        "pool_scale": 1.0 + nrm(ks[11], (L, POOL_WIDTH), 0.1),
        "w_lift_a": nrm(ks[12], (L, DN_WIDTH, D), DN_WIDTH ** -0.5),
        "w_lift_b": nrm(ks[13], (L, POOL_WIDTH, D), POOL_WIDTH ** -0.5),
        "w_out": nrm(ks[14], (L, D, D), D ** -0.5),
        "norm2_g": 1.0 + nrm(ks[15], (L, D), 0.1),
        "w_router_group": nrm(ks[16], (L, D, MOE_GROUPS), D ** -0.5),
        "b_router_group": nrm(ks[17], (L, MOE_GROUPS), 0.01),
        "w_router_expert": nrm(ks[18], (L, D, MOE_EXPERTS), D ** -0.5),
        "b_router_expert": nrm(ks[19], (L, MOE_EXPERTS), 0.01),
        "w_gate": nrm(ks[20], (L, MOE_EXPERTS, D, MOE_D_FF), D ** -0.5),
        "w_up": nrm(ks[21], (L, MOE_EXPERTS, D, MOE_D_FF), D ** -0.5),
        "w_down": nrm(ks[22], (L, MOE_EXPERTS, MOE_D_FF, D), MOE_D_FF ** -0.5),
        "final_norm_g": 1.0 + nrm(ks[23], (D,), 0.1),
    }


def reference(x, c, w_ada, b_ada, norm1_g, w_in, conv_w, a_log, dt_bias, dn_norm_g, pool_w, pool_scale,
              w_lift_a, w_lift_b, w_out, norm2_g, w_router_group, b_router_group, w_router_expert,
              b_router_expert, w_gate, w_up, w_down, final_norm_g):
    h = x
    c_act = jax.nn.silu(c)
    for l in range(DEPTH):
        mod = c_act @ w_ada[l] + b_ada[l]
        sh1, sc1, gt1, sh2, sc2, gt2 = [m[:, None, :] for m in jnp.split(mod, N_MOD, axis=-1)]
        xn = _rmsnorm(h, norm1_g[l]) * (1 + sc1) + sh1
        h = h + gt1 * _token_mixer(xn, w_in[l], conv_w[l], a_log[l], dt_bias[l], dn_norm_g[l], pool_w[l],
                                   pool_scale[l], w_lift_a[l], w_lift_b[l], w_out[l])
        xn = _rmsnorm(h, norm2_g[l]) * (1 + sc2) + sh2
        h = h + gt2 * _hier_moe(xn, w_router_group[l], b_router_group[l], w_router_expert[l],
                                b_router_expert[l], w_gate[l], w_up[l], w_down[l])
    return _rmsnorm(h, final_norm_g)
```

```python
import functools

import jax
import jax.numpy as jnp
from jax import lax
from jax.experimental import pallas as pl
from jax.experimental.pallas import tpu as pltpu

F32 = jnp.float32
BF16 = jnp.bfloat16
HIGHEST = lax.Precision.HIGHEST

EPS = 1e-6
LANES = 128
SUBLANES = 8
DN_HEADS = 4
DN_HEAD_DIM = 128
DN_WIDTH = DN_HEADS * DN_HEAD_DIM
DN_CONV = 4
DN_CHUNK = 64
POOL_WINDOWS = (2, 4, 8, 16)
POOL_GROUP_DIM = 128
POOL_WIDTH = POOL_GROUP_DIM * len(POOL_WINDOWS)
POOL_HIST = 16
MOE_GROUPS = 4
MOE_EXPERTS_PER_GROUP = 8
MOE_EXPERTS = MOE_GROUPS * MOE_EXPERTS_PER_GROUP
MOE_ROWS = 256
VMEM_LIMIT = 56 * 1024 * 1024


def _sigmoid(x):
    return 1.0 / (1.0 + jnp.exp(-x))


def _silu(x):
    return x * _sigmoid(x)


def _dot(a, b):
    return jnp.dot(a, b, preferred_element_type=F32)


def _dot_f32(a, b):
    return jnp.dot(a, b, preferred_element_type=F32, precision=HIGHEST)


def _dot_nt(a, b):
    return lax.dot_general(a, b, (((1,), (1,)), ((), ())), preferred_element_type=F32)


def _dot_tn(a, b):
    return lax.dot_general(a, b, (((0,), (0,)), ((), ())), preferred_element_type=F32)


def _split_dot(a, b):
    a_hi = a.astype(BF16)
    a_lo = (a - a_hi.astype(F32)).astype(BF16)
    b_hi = b.astype(BF16)
    b_lo = (b - b_hi.astype(F32)).astype(BF16)
    return _dot(a_hi, b_hi) + (_dot(a_hi, b_lo) + _dot(a_lo, b_hi))


def _rms_scale(x):
    return x * lax.rsqrt(jnp.mean(x * x, axis=-1, keepdims=True) + EPS)


def _ada_kernel(c_ref, w_ref, b_ref, o_ref):
    o_ref[...] = _dot_f32(_silu(c_ref[...]), w_ref[...]) + b_ref[...]


def _ada(c, w_ada, b_ada):
    B, D = c.shape
    n_out = w_ada.shape[1]
    tn = D
    return pl.pallas_call(
        _ada_kernel,
        out_shape=jax.ShapeDtypeStruct((B, n_out), F32),
        grid=(n_out // tn,),
        in_specs=[pl.BlockSpec((B, D), lambda j: (0, 0)),
                  pl.BlockSpec((D, tn), lambda j: (0, j)),
                  pl.BlockSpec((1, tn), lambda j: (0, j))],
        out_specs=pl.BlockSpec((B, tn), lambda j: (0, j)),
        compiler_params=pltpu.CompilerParams(dimension_semantics=("arbitrary",),
                                             vmem_limit_bytes=VMEM_LIMIT),
        name="ada",
    )(c, w_ada, b_ada.reshape(1, n_out))


def _inproj_kernel(x_ref, g_ref, sc_ref, sh_ref, wqkvz_ref, wab_ref, wpu_ref, wg_ref,
                   qkv_ref, z_ref, ab_ref, pu_ref, ga_ref, gb_ref):
    x = x_ref[...]
    xn = _rms_scale(x) * g_ref[...] * (1.0 + sc_ref[0]) + sh_ref[0]
    xb = xn.astype(BF16)
    n_qkv = qkv_ref.shape[1]
    d = ga_ref.shape[1]
    qkvz = _dot(xb, wqkvz_ref[...])
    qkv_ref[...] = qkvz[:, :n_qkv]
    z_ref[...] = qkvz[:, n_qkv:]
    ab_ref[...] = _dot(xb, wab_ref[...])
    pu_ref[...] = _dot(xb, wpu_ref[...])
    gates = _dot(xb, wg_ref[...])
    ga_ref[...] = gates[:, :d]
    gb_ref[...] = gates[:, d:]


def _inproj(x2, g, sc, sh, w_qkvz, w_ab, w_pu, w_g, seq, tm):
    N, D = x2.shape
    per_seq = seq // tm
    n_qkv = 3 * DN_WIDTH

    def row(i):
        return (i, 0)

    def mod(i):
        return (i // per_seq, 0, 0)

    def const(i):
        return (0, 0)

    outs = [(N, n_qkv), (N, DN_WIDTH), (N, LANES), (N, POOL_WIDTH), (N, D), (N, D)]
    return pl.pallas_call(
        _inproj_kernel,
        out_shape=[jax.ShapeDtypeStruct(s, F32) for s in outs],
        grid=(N // tm,),
        in_specs=[pl.BlockSpec((tm, D), row),
                  pl.BlockSpec((1, D), const),
                  pl.BlockSpec((1, 1, D), mod),
                  pl.BlockSpec((1, 1, D), mod),
                  pl.BlockSpec(w_qkvz.shape, const),
                  pl.BlockSpec(w_ab.shape, const),
                  pl.BlockSpec(w_pu.shape, const),
                  pl.BlockSpec(w_g.shape, const)],
        out_specs=[pl.BlockSpec((tm, s[1]), row) for s in outs],
        compiler_params=pltpu.CompilerParams(dimension_semantics=("arbitrary",),
                                             vmem_limit_bytes=VMEM_LIMIT),
        name="inproj",
    )(x2, g, sc, sh, w_qkvz, w_ab, w_pu, w_g)


def _deltanet_kernel(qkv_ref, prev_ref, z_ref, ab_ref, convw_ref, alog_ref, dtb_ref, dng_ref,
                     ya_ref, state_ref):
    s = pl.program_id(1)
    ts = qkv_ref.shape[0]
    C = DN_CHUNK
    hd = DN_HEAD_DIM

    @pl.when(s == 0)
    def _():
        state_ref[...] = jnp.zeros_like(state_ref)

    tile = qkv_ref[...]
    prev = jnp.where(s == 0, 0.0, prev_ref[...])
    xf = jnp.concatenate([prev, tile], axis=0)
    cw = convw_ref[...]
    acc = tile * cw[DN_CONV - 1:DN_CONV]
    for sft in range(1, DN_CONV):
        shifted = pltpu.roll(xf, sft, 0)[SUBLANES:SUBLANES + ts]
        acc = acc + shifted * cw[DN_CONV - 1 - sft:DN_CONV - sft]
    act = _silu(acc)

    ab = ab_ref[...]
    neg_a = -jnp.exp(alog_ref[...])
    xs = ab + dtb_ref[...]
    softplus = jnp.maximum(xs, 0.0) + jnp.log1p(jnp.exp(-jnp.abs(xs)))
    g_all = neg_a * softplus
    beta_all = _sigmoid(ab)

    ri = lax.broadcasted_iota(jnp.int32, (C, C), 0)
    ci = lax.broadcasted_iota(jnp.int32, (C, C), 1)
    causal = ri >= ci
    strict = ri > ci
    tril = jnp.where(causal, 1.0, 0.0).astype(F32)
    eye = jnp.where(ri == ci, 1.0, 0.0).astype(F32)
    zg = _silu(z_ref[...])
    dng = dng_ref[...]

    for c in range(ts // C):
        r0 = c * C
        gcum = _dot_f32(tril, g_all[r0:r0 + C])
        gcum_t = gcum.T
        g_last = gcum[C - 1:C]
        e_g = jnp.exp(gcum)
        e_kd = jnp.exp(g_last - gcum)
        e_last = jnp.exp(g_last)
        for h in range(DN_HEADS):
            q = act[r0:r0 + C, h * hd:(h + 1) * hd]
            k = act[r0:r0 + C, DN_WIDTH + h * hd:DN_WIDTH + (h + 1) * hd]
            v = act[r0:r0 + C, 2 * DN_WIDTH + h * hd:2 * DN_WIDTH + (h + 1) * hd]
            q = q * lax.rsqrt(jnp.sum(q * q, axis=-1, keepdims=True) + EPS) * (hd ** -0.5)
            k = k * lax.rsqrt(jnp.sum(k * k, axis=-1, keepdims=True) + EPS)
            beta = beta_all[r0:r0 + C, DN_HEADS + h:DN_HEADS + h + 1]
            gc_col = gcum[:, h:h + 1]
            gc_row = gcum_t[h:h + 1, :]
            diff = jnp.where(causal, gc_col - gc_row, 0.0)
            decay = jnp.where(causal, jnp.exp(diff), 0.0)
            eg = e_g[:, h:h + 1]
            ekd = e_kd[:, h:h + 1]
            el = e_last[:, h:h + 1]

            kb = k * beta
            vb = v * beta
            k16 = k.astype(BF16)
            lower = jnp.where(strict, _dot_nt(kb.astype(BF16), k16) * decay, 0.0)
            p = -lower
            t = eye + p
            for _ in range(5):
                p = _split_dot(p, p)
                t = t + _split_dot(p, t)
            t16 = t.astype(BF16)
            u = _dot(t16, vb.astype(BF16))
            w = _dot(t16, (kb * eg).astype(BF16))
            attn = _dot_nt(q.astype(BF16), k16) * decay

            st = state_ref[h]
            st16 = st.astype(BF16)
            v_new = u - _dot(w.astype(BF16), st16)
            vn16 = v_new.astype(BF16)
            o = _dot((q * eg).astype(BF16), st16) + _dot(attn.astype(BF16), vn16)
            state_ref[h] = st * el + _dot_tn((k * ekd).astype(BF16), vn16)

            y = _rms_scale(o) * dng * zg[r0:r0 + C, h * hd:(h + 1) * hd]
            ya_ref[r0:r0 + C, h * hd:(h + 1) * hd] = y.astype(ya_ref.dtype)


def _deltanet(qkv, z, ab, conv_w, a_log, dt_bias, dn_norm_g, batch, seq, ts):
    N = qkv.shape[0]
    per_seq = seq // ts
    n_qkv = qkv.shape[1]
    pad = LANES - DN_HEADS
    alog_row = jnp.pad(a_log, (0, pad)).reshape(1, LANES)
    dtb_row = jnp.pad(dt_bias, (0, pad)).reshape(1, LANES)

    def row(b, s):
        return (b * per_seq + s, 0)

    def prev(b, s):
        return (jnp.maximum((b * per_seq + s) * (ts // SUBLANES) - 1, 0), 0)

    def const(b, s):
        return (0, 0)

    return pl.pallas_call(
        _deltanet_kernel,
        out_shape=jax.ShapeDtypeStruct((N, DN_WIDTH), BF16),
        grid=(batch, per_seq),
        in_specs=[pl.BlockSpec((ts, n_qkv), row),
                  pl.BlockSpec((SUBLANES, n_qkv), prev),
                  pl.BlockSpec((ts, DN_WIDTH), row),
                  pl.BlockSpec((ts, LANES), row),
                  pl.BlockSpec(conv_w.shape, const),
                  pl.BlockSpec((1, LANES), const),
                  pl.BlockSpec((1, LANES), const),
                  pl.BlockSpec((1, DN_HEAD_DIM), const)],
        out_specs=pl.BlockSpec((ts, DN_WIDTH), row),
        scratch_shapes=[pltpu.VMEM((DN_HEADS, DN_HEAD_DIM, DN_HEAD_DIM), F32)],
        compiler_params=pltpu.CompilerParams(dimension_semantics=("arbitrary", "arbitrary"),
                                             vmem_limit_bytes=VMEM_LIMIT),
        name="deltanet",
    )(qkv, qkv, z, ab, conv_w, alog_row, dtb_row, dn_norm_g.reshape(1, DN_HEAD_DIM))


def _mix_kernel(ya_ref, pu_ref, pprev_ref, ga_ref, gb_ref, x_ref, gt1_ref, sc2_ref, sh2_ref, g2_ref,
                poolw_ref, pscale_ref, wla_ref, wlb_ref, wout_ref, wr_ref, br_ref,
                h_ref, xn_ref, route_ref, counts_ref, carry_ref, *, per_seq):
    i = pl.program_id(0)
    tm = x_ref.shape[0]
    first = (i % per_seq) == 0

    @pl.when(i == 0)
    def _():
        carry_ref[...] = jnp.zeros_like(carry_ref)

    pu = pu_ref[...]
    hist = jnp.where(first, 0.0, pprev_ref[...])
    pf = jnp.concatenate([hist, pu], axis=0)
    pos = (i % per_seq) * tm + lax.broadcasted_iota(jnp.int32, (tm, 1), 0)
    pscale = pscale_ref[...]
    yb_parts = []
    for gi, win in enumerate(POOL_WINDOWS):
        lo = gi * POOL_GROUP_DIM
        xg = pf[:, lo:lo + POOL_GROUP_DIM]
        acc = xg
        span = 1
        while span < win:
            acc = acc + pltpu.roll(acc, span, 0)
            span *= 2
        cnt = jnp.minimum(pos + 1, win).astype(F32)
        pooled = acc[POOL_HIST:] / cnt - pu[:, lo:lo + POOL_GROUP_DIM]
        yg = _dot(pooled.astype(BF16), poolw_ref[gi])
        yb_parts.append(yg * pscale[:, lo:lo + POOL_GROUP_DIM])
    yb = jnp.concatenate(yb_parts, axis=1)

    lift_a = _dot(ya_ref[...], wla_ref[...])
    lift_b = _dot(yb.astype(BF16), wlb_ref[...])
    mixed = _sigmoid(ga_ref[...]) * lift_a + _sigmoid(gb_ref[...]) * lift_b
    h = x_ref[...] + gt1_ref[0] * _dot(mixed.astype(BF16), wout_ref[...])
    h_ref[...] = h
    xn = _rms_scale(h) * g2_ref[...] * (1.0 + sc2_ref[0]) + sh2_ref[0]
    xn_ref[...] = xn

    logits = _dot_f32(xn, wr_ref[...]) + br_ref[...]
    li = lax.broadcasted_iota(jnp.int32, (tm, LANES), 1).astype(F32)
    neg = -jnp.inf
    big = float(LANES)
    is_g = li < MOE_GROUPS
    lg = jnp.where(is_g, logits, neg)
    mg = jnp.max(lg, axis=-1, keepdims=True)
    p_grp = 1.0 / jnp.sum(jnp.where(is_g, jnp.exp(lg - mg), 0.0), axis=-1, keepdims=True)
    grp = jnp.min(jnp.where(lg == mg, li, big), axis=-1, keepdims=True)
    e_lo = MOE_GROUPS + MOE_EXPERTS_PER_GROUP * grp
    is_e = (li >= e_lo) & (li < e_lo + MOE_EXPERTS_PER_GROUP)
    le = jnp.where(is_e, logits, neg)
    m1 = jnp.max(le, axis=-1, keepdims=True)
    i1 = jnp.min(jnp.where(le == m1, li, big), axis=-1, keepdims=True)
    le2 = jnp.where(li == i1, neg, le)
    m2 = jnp.max(le2, axis=-1, keepdims=True)
    i2 = jnp.min(jnp.where(le2 == m2, li, big), axis=-1, keepdims=True)
    e2 = jnp.exp(m2 - m1)
    w1 = p_grp / (1.0 + e2)
    w2 = p_grp * e2 / (1.0 + e2)
    eid1 = i1 - MOE_GROUPS
    eid2 = i2 - MOE_GROUPS

    oh1 = li == eid1
    oh2 = li == eid2
    onehot = jnp.where(oh1 | oh2, 1.0, 0.0)
    ri = lax.broadcasted_iota(jnp.int32, (tm, tm), 0)
    ci = lax.broadcasted_iota(jnp.int32, (tm, tm), 1)
    below = jnp.where(ri > ci, 1.0, 0.0).astype(BF16)
    carry = carry_ref[0:1, :]
    before = _dot(below, onehot.astype(BF16)) + carry
    rank1 = jnp.sum(jnp.where(oh1, before, 0.0), axis=-1, keepdims=True)
    rank2 = jnp.sum(jnp.where(oh2, before, 0.0), axis=-1, keepdims=True)
    new_carry = carry + jnp.sum(onehot, axis=0, keepdims=True)
    carry_ref[...] = jnp.broadcast_to(new_carry, carry_ref.shape)
    counts_ref[...] = jnp.broadcast_to(new_carry, counts_ref.shape)

    route = jnp.where(li == 0.0, w1, 0.0)
    for lane, val in ((1, w2), (2, eid1), (3, eid2), (4, rank1), (5, rank2)):
        route = jnp.where(li == float(lane), val, route)
    route_ref[...] = route


def _mix(ya, pu, ga, gb, x2, gt1, sc2, sh2, g2, pool_w, pool_scale, w_lift_a, w_lift_b, w_out,
         w_router, b_router, seq, tm):
    N, D = x2.shape
    per_seq = seq // tm

    def row(i):
        return (i, 0)

    def prev(i):
        return (jnp.maximum(i * (tm // POOL_HIST) - 1, 0), 0)

    def mod(i):
        return (i // per_seq, 0, 0)

    def const(i):
        return (0, 0)

    return pl.pallas_call(
        functools.partial(_mix_kernel, per_seq=per_seq),
        out_shape=[jax.ShapeDtypeStruct((N, D), F32),
                   jax.ShapeDtypeStruct((N, D), F32),
                   jax.ShapeDtypeStruct((N, LANES), F32),
                   jax.ShapeDtypeStruct((SUBLANES, LANES), F32)],
        grid=(N // tm,),
        in_specs=[pl.BlockSpec((tm, DN_WIDTH), row),
                  pl.BlockSpec((tm, POOL_WIDTH), row),
                  pl.BlockSpec((POOL_HIST, POOL_WIDTH), prev),
                  pl.BlockSpec((tm, D), row),
                  pl.BlockSpec((tm, D), row),
                  pl.BlockSpec((tm, D), row),
                  pl.BlockSpec((1, 1, D), mod),
                  pl.BlockSpec((1, 1, D), mod),
                  pl.BlockSpec((1, 1, D), mod),
                  pl.BlockSpec((1, D), const),
                  pl.BlockSpec(pool_w.shape, lambda i: (0, 0, 0)),
                  pl.BlockSpec((1, POOL_WIDTH), const),
                  pl.BlockSpec(w_lift_a.shape, const),
                  pl.BlockSpec(w_lift_b.shape, const),
                  pl.BlockSpec(w_out.shape, const),
                  pl.BlockSpec(w_router.shape, const),
                  pl.BlockSpec((1, LANES), const)],
        out_specs=[pl.BlockSpec((tm, D), row),
                   pl.BlockSpec((tm, D), row),
                   pl.BlockSpec((tm, LANES), row),
                   pl.BlockSpec((SUBLANES, LANES), const)],
        scratch_shapes=[pltpu.VMEM((SUBLANES, LANES), F32)],
        compiler_params=pltpu.CompilerParams(dimension_semantics=("arbitrary",),
                                             vmem_limit_bytes=VMEM_LIMIT),
        name="mix",
    )(ya, pu, pu, ga, gb, x2, gt1, sc2, sh2, g2, pool_w, pool_scale, w_lift_a, w_lift_b, w_out,
      w_router, b_router)


def _dest_kernel(route_ref, pstart_ref, dest_ref):
    route = route_ref[...]
    tm = route.shape[0]
    li = lax.broadcasted_iota(jnp.int32, (tm, LANES), 1).astype(F32)
    pstart = pstart_ref[...]
    out = jnp.zeros((tm, LANES), F32)
    for k in range(2):
        eid = route[:, 2 + k:3 + k]
        rank = route[:, 4 + k:5 + k]
        slot = jnp.sum(jnp.where(li == eid, pstart, 0.0), axis=-1, keepdims=True) + rank
        out = jnp.where(li == float(k), slot, out)
    dest_ref[...] = out.astype(jnp.int32)


def _dest(route, pstart_row, tm):
    N = route.shape[0]
    return pl.pallas_call(
        _dest_kernel,
        out_shape=jax.ShapeDtypeStruct((N, LANES), jnp.int32),
        grid=(N // tm,),
        in_specs=[pl.BlockSpec((tm, LANES), lambda i: (i, 0)),
                  pl.BlockSpec((1, LANES), lambda i: (0, 0))],
        out_specs=pl.BlockSpec((tm, LANES), lambda i: (i, 0)),
        compiler_params=pltpu.CompilerParams(dimension_semantics=("arbitrary",)),
        name="dest",
    )(route, pstart_row)


def _dispatch_kernel(d0_ref, d1_ref, x_ref, xs_in_ref, xs_ref, sem):
    del xs_in_ref
    tm = x_ref.shape[0]

    def row_copy(t, d):
        return pltpu.make_async_copy(x_ref.at[pl.ds(t, 1)], xs_ref.at[pl.ds(d, 1)], sem)

    def issue(t, carry):
        row_copy(t, d0_ref[0, t]).start()
        row_copy(t, d1_ref[0, t]).start()
        return carry

    lax.fori_loop(0, tm, issue, 0)

    def drain(t, carry):
        row_copy(t, d0_ref[0, t]).wait()
        row_copy(t, d1_ref[0, t]).wait()
        return carry

    lax.fori_loop(0, tm, drain, 0)


def _dispatch(d0, d1, xn, n_slots, tm):
    N, D = xn.shape
    nt = N // tm
    xs0 = jnp.zeros((n_slots, D), xn.dtype)
    smem = functools.partial(pl.BlockSpec, (None, 1, tm), lambda i: (i, 0, 0),
                             memory_space=pltpu.SMEM)
    return pl.pallas_call(
        _dispatch_kernel,
        out_shape=jax.ShapeDtypeStruct((n_slots, D), xn.dtype),
        grid=(nt,),
        in_specs=[smem(), smem(),
                  pl.BlockSpec((tm, D), lambda i: (i, 0)),
                  pl.BlockSpec(memory_space=pl.ANY)],
        out_specs=pl.BlockSpec(memory_space=pl.ANY),
        scratch_shapes=[pltpu.SemaphoreType.DMA],
        input_output_aliases={3: 0},
        compiler_params=pltpu.CompilerParams(dimension_semantics=("arbitrary",),
                                             vmem_limit_bytes=VMEM_LIMIT),
        name="dispatch",
    )(d0.reshape(nt, 1, tm), d1.reshape(nt, 1, tm), xn, xs0)


def _experts_kernel(blk_e_ref, nused_ref, x_ref, wg_ref, wu_ref, wd_ref, y_ref):
    del blk_e_ref
    used = pl.program_id(0) < nused_ref[0]

    @pl.when(used)
    def _():
        x = x_ref[...].astype(BF16)
        hid = _silu(_dot(x, wg_ref[...])) * _dot(x, wu_ref[...])
        y_ref[...] = _dot(hid.astype(BF16), wd_ref[...])

    @pl.when(jnp.logical_not(used))
    def _():
        y_ref[...] = jnp.zeros_like(y_ref)


def _experts(blk_e, nused, xs, w_gate, w_up, w_down):
    P, D = xs.shape
    nb = P // MOE_ROWS
    dff = w_gate.shape[2]

    def rows(j, blk_e_ref, nused_ref):
        return (jnp.minimum(j, nused_ref[0] - 1), 0)

    def wsel(j, blk_e_ref, nused_ref):
        return (blk_e_ref[j], 0, 0)

    return pl.pallas_call(
        _experts_kernel,
        out_shape=jax.ShapeDtypeStruct((P, D), F32),
        grid_spec=pltpu.PrefetchScalarGridSpec(
            num_scalar_prefetch=2,
            grid=(nb,),
            in_specs=[pl.BlockSpec((MOE_ROWS, D), rows),
                      pl.BlockSpec((None, D, dff), wsel),
                      pl.BlockSpec((None, D, dff), wsel),
                      pl.BlockSpec((None, dff, D), wsel)],
            out_specs=pl.BlockSpec((MOE_ROWS, D), lambda j, blk_e_ref, nused_ref: (j, 0))),
        compiler_params=pltpu.CompilerParams(dimension_semantics=("arbitrary",),
                                             vmem_limit_bytes=VMEM_LIMIT),
        name="experts",
    )(blk_e, nused, xs, w_gate, w_up, w_down)


def _combine_kernel(d0_ref, d1_ref, route_ref, h_ref, gt2_ref, gf_ref, y_ref, o_ref,
                    buf0_ref, buf1_ref, sem):
    tm = h_ref.shape[0]

    def row_copy(buf_ref, t, d):
        return pltpu.make_async_copy(y_ref.at[pl.ds(d, 1)], buf_ref.at[pl.ds(t, 1)], sem)

    def issue(t, carry):
        row_copy(buf0_ref, t, d0_ref[0, t]).start()
        row_copy(buf1_ref, t, d1_ref[0, t]).start()
        return carry

    lax.fori_loop(0, tm, issue, 0)

    def drain(t, carry):
        row_copy(buf0_ref, t, d0_ref[0, t]).wait()
        row_copy(buf1_ref, t, d1_ref[0, t]).wait()
        return carry

    lax.fori_loop(0, tm, drain, 0)

    route = route_ref[...]
    moe = buf0_ref[...] * route[:, 0:1] + buf1_ref[...] * route[:, 1:2]
    h = h_ref[...] + gt2_ref[0] * moe
    o_ref[...] = _rms_scale(h) * gf_ref[...]


def _combine(d0, d1, route, h, gt2, gf, y, seq, tm):
    N, D = h.shape
    nt = N // tm
    per_seq = seq // tm
    smem = functools.partial(pl.BlockSpec, (None, 1, tm), lambda i: (i, 0, 0),
                             memory_space=pltpu.SMEM)
    return pl.pallas_call(
        _combine_kernel,
        out_shape=jax.ShapeDtypeStruct((N, D), F32),
        grid=(nt,),
        in_specs=[smem(), smem(),
                  pl.BlockSpec((tm, LANES), lambda i: (i, 0)),
                  pl.BlockSpec((tm, D), lambda i: (i, 0)),
                  pl.BlockSpec((1, 1, D), lambda i: (i // per_seq, 0, 0)),
                  pl.BlockSpec((1, D), lambda i: (0, 0)),
                  pl.BlockSpec(memory_space=pl.ANY)],
        out_specs=pl.BlockSpec((tm, D), lambda i: (i, 0)),
        scratch_shapes=[pltpu.VMEM((tm, D), F32), pltpu.VMEM((tm, D), F32), pltpu.SemaphoreType.DMA],
        compiler_params=pltpu.CompilerParams(dimension_semantics=("arbitrary",),
                                             vmem_limit_bytes=VMEM_LIMIT),
        name="combine",
    )(d0.reshape(nt, 1, tm), d1.reshape(nt, 1, tm), route, h, gt2, gf, y)


def _layer(h2, mod, batch, seq, norm1_g, w_in, conv_w, a_log, dt_bias, dn_norm_g, pool_w, pool_scale,
           w_lift_a, w_lift_b, w_out, norm2_g, w_rg, b_rg, w_re, b_re, w_gate, w_up, w_down, final_g):
    N, D = h2.shape
    sh1, sc1, gt1, sh2, sc2, gt2 = [m.reshape(batch, 1, D) for m in jnp.split(mod, 6, axis=-1)]

    o_z = 3 * DN_WIDTH
    o_a = o_z + DN_WIDTH
    o_pu = o_a + 2 * DN_HEADS
    o_g = o_pu + POOL_WIDTH
    w_qkvz = w_in[:, :o_a].astype(BF16)
    w_ab = jnp.pad(w_in[:, o_a:o_pu], ((0, 0), (0, LANES - 2 * DN_HEADS))).astype(BF16)
    w_pu = w_in[:, o_pu:o_g].astype(BF16)
    w_g = w_in[:, o_g:].astype(BF16)

    tm = min(256, seq)
    qkv, z, ab, pu, ga, gb = _inproj(h2, norm1_g.reshape(1, D), sc1, sh1, w_qkvz, w_ab, w_pu, w_g, seq, tm)

    ts = min(128, seq)
    ya = _deltanet(qkv, z, ab, conv_w, a_log, dt_bias, dn_norm_g, batch, seq, ts)

    n_r = MOE_GROUPS + MOE_EXPERTS
    w_router = jnp.pad(jnp.concatenate([w_rg, w_re], axis=1), ((0, 0), (0, LANES - n_r)))
    b_router = jnp.pad(jnp.concatenate([b_rg, b_re]), (0, LANES - n_r)).reshape(1, LANES)
    h_mid, xn2, route, counts = _mix(
        ya, pu, ga, gb, h2, gt1, sc2, sh2, norm2_g.reshape(1, D), pool_w.astype(BF16),
        pool_scale.reshape(1, POOL_WIDTH), w_lift_a.astype(BF16), w_lift_b.astype(BF16),
        w_out.astype(BF16), w_router, b_router, seq, tm)

    cnt = counts[0, :MOE_EXPERTS].astype(jnp.int32)
    padded = (cnt + MOE_ROWS - 1) // MOE_ROWS * MOE_ROWS
    pend = jnp.cumsum(padded)
    pstart = pend - padded
    n_slots = 2 * N + MOE_EXPERTS * MOE_ROWS
    nb = n_slots // MOE_ROWS
    nused = (pend[-1] // MOE_ROWS).astype(jnp.int32).reshape(1)
    blk_e = jnp.minimum(jnp.searchsorted(pend, jnp.arange(nb, dtype=jnp.int32) * MOE_ROWS, side='right'),
                        MOE_EXPERTS - 1).astype(jnp.int32)
    pstart_row = jnp.pad(pstart.astype(F32), (0, LANES - MOE_EXPERTS)).reshape(1, LANES)

    dest = _dest(route, pstart_row, tm)
    d0 = dest[:, 0]
    d1 = dest[:, 1]
    xs = _dispatch(d0, d1, xn2, n_slots, tm)
    y = _experts(blk_e, nused, xs, w_gate.astype(BF16), w_up.astype(BF16), w_down.astype(BF16))
    return _combine(d0, d1, route, h_mid, gt2, final_g.reshape(1, D), y, seq, tm)


def kernel(x, c, w_ada, b_ada, norm1_g, w_in, conv_w, a_log, dt_bias, dn_norm_g, pool_w, pool_scale,
           w_lift_a, w_lift_b, w_out, norm2_g, w_router_group, b_router_group, w_router_expert,
           b_router_expert, w_gate, w_up, w_down, final_norm_g):
    batch, seq, D = x.shape
    depth = w_ada.shape[0]
    assert depth == 1, "the combine kernel fuses the final rmsnorm, so exactly one layer is supported"
    assert seq % DN_CHUNK == 0 and D % LANES == 0
    h2 = x.reshape(batch * seq, D)
    l = 0
    mod = _ada(c, w_ada[l], b_ada[l])
    out = _layer(h2, mod, batch, seq, norm1_g[l], w_in[l], conv_w[l], a_log[l], dt_bias[l], dn_norm_g[l],
                 pool_w[l], pool_scale[l], w_lift_a[l], w_lift_b[l], w_out[l], norm2_g[l],
                 w_router_group[l], b_router_group[l], w_router_expert[l], b_router_expert[l],
                 w_gate[l], w_up[l], w_down[l], final_norm_g)
    return out.reshape(batch, seq, D)
```

```python
import functools

import jax
import jax.numpy as jnp
from jax import lax
from jax.experimental import pallas as pl
from jax.experimental.pallas import tpu as pltpu

F32 = jnp.float32
BF16 = jnp.bfloat16
HIGHEST = lax.Precision.HIGHEST

EPS = 1e-6
LANES = 128
SUBLANES = 8
DN_HEADS = 4
DN_HEAD_DIM = 128
DN_WIDTH = DN_HEADS * DN_HEAD_DIM
DN_CONV = 4
DN_CHUNK = 64
POOL_WINDOWS = (2, 4, 8, 16)
POOL_GROUP_DIM = 128
POOL_WIDTH = POOL_GROUP_DIM * len(POOL_WINDOWS)
POOL_HIST = 16
MOE_GROUPS = 4
MOE_EXPERTS_PER_GROUP = 8
MOE_EXPERTS = MOE_GROUPS * MOE_EXPERTS_PER_GROUP
MOE_ROWS = 256
VMEM_LIMIT = 56 * 1024 * 1024


def _sigmoid(x):
    return 1.0 / (1.0 + jnp.exp(-x))


def _silu(x):
    return x * _sigmoid(x)


def _dot(a, b):
    return jnp.dot(a, b, preferred_element_type=F32)


def _dot_f32(a, b):
    return jnp.dot(a, b, preferred_element_type=F32, precision=HIGHEST)


def _dot_nt(a, b):
    return lax.dot_general(a, b, (((1,), (1,)), ((), ())), preferred_element_type=F32)


def _dot_tn(a, b):
    return lax.dot_general(a, b, (((0,), (0,)), ((), ())), preferred_element_type=F32)


def _split_dot(a, b):
    a_hi = a.astype(BF16)
    a_lo = (a - a_hi.astype(F32)).astype(BF16)
    b_hi = b.astype(BF16)
    b_lo = (b - b_hi.astype(F32)).astype(BF16)
    return _dot(a_hi, b_hi) + (_dot(a_hi, b_lo) + _dot(a_lo, b_hi))


def _rms_scale(x):
    return x * lax.rsqrt(jnp.mean(x * x, axis=-1, keepdims=True) + EPS)


def _ada_kernel(c_ref, w_ref, b_ref, o_ref):
    o_ref[...] = _dot_f32(_silu(c_ref[...]), w_ref[...]) + b_ref[...]


def _ada(c, w_ada, b_ada):
    B, D = c.shape
    n_out = w_ada.shape[1]
    tn = D
    return pl.pallas_call(
        _ada_kernel,
        out_shape=jax.ShapeDtypeStruct((B, n_out), F32),
        grid=(n_out // tn,),
        in_specs=[pl.BlockSpec((B, D), lambda j: (0, 0)),
                  pl.BlockSpec((D, tn), lambda j: (0, j)),
                  pl.BlockSpec((1, tn), lambda j: (0, j))],
        out_specs=pl.BlockSpec((B, tn), lambda j: (0, j)),
        compiler_params=pltpu.CompilerParams(dimension_semantics=("arbitrary",),
                                             vmem_limit_bytes=VMEM_LIMIT),
        name="ada",
    )(c, w_ada, b_ada.reshape(1, n_out))


def _inproj_kernel(x_ref, g_ref, sc_ref, sh_ref, wqkvz_ref, wab_ref, wpu_ref, wg_ref,
                   qkv_ref, z_ref, ab_ref, pu_ref, ga_ref, gb_ref):
    x = x_ref[...]
    xn = _rms_scale(x) * g_ref[...] * (1.0 + sc_ref[0]) + sh_ref[0]
    xb = xn.astype(BF16)
    n_qkv = qkv_ref.shape[1]
    d = ga_ref.shape[1]
    qkvz = _dot(xb, wqkvz_ref[...])
    qkv_ref[...] = qkvz[:, :n_qkv]
    z_ref[...] = qkvz[:, n_qkv:]
    ab_ref[...] = _dot(xb, wab_ref[...])
    pu_ref[...] = _dot(xb, wpu_ref[...])
    gates = _dot(xb, wg_ref[...])
    ga_ref[...] = gates[:, :d]
    gb_ref[...] = gates[:, d:]


def _inproj(x2, g, sc, sh, w_qkvz, w_ab, w_pu, w_g, seq, tm):
    N, D = x2.shape
    per_seq = seq // tm
    n_qkv = 3 * DN_WIDTH

    def row(i):
        return (i, 0)

    def mod(i):
        return (i // per_seq, 0, 0)

    def const(i):
        return (0, 0)

    outs = [(N, n_qkv), (N, DN_WIDTH), (N, LANES), (N, POOL_WIDTH), (N, D), (N, D)]
    return pl.pallas_call(
        _inproj_kernel,
        out_shape=[jax.ShapeDtypeStruct(s, F32) for s in outs],
        grid=(N // tm,),
        in_specs=[pl.BlockSpec((tm, D), row),
                  pl.BlockSpec((1, D), const),
                  pl.BlockSpec((1, 1, D), mod),
                  pl.BlockSpec((1, 1, D), mod),
                  pl.BlockSpec(w_qkvz.shape, const),
                  pl.BlockSpec(w_ab.shape, const),
                  pl.BlockSpec(w_pu.shape, const),
                  pl.BlockSpec(w_g.shape, const)],
        out_specs=[pl.BlockSpec((tm, s[1]), row) for s in outs],
        compiler_params=pltpu.CompilerParams(dimension_semantics=("arbitrary",),
                                             vmem_limit_bytes=VMEM_LIMIT),
        name="inproj",
    )(x2, g, sc, sh, w_qkvz, w_ab, w_pu, w_g)


def _unit_lower_inverses(lowers, eye, ri, ci):
    def mm(a, b):
        return _dot(a.astype(BF16), b.astype(BF16))

    base = 3
    same = (ri >> base) == (ci >> base)
    ld = [jnp.where(same, l, 0.0) for l in lowers]
    a0 = [eye - x for x in ld]
    p1 = [mm(x, x) for x in ld]
    x1 = [a + mm(a, p) for a, p in zip(a0, p1)]
    p2 = [mm(p, p) for p in p1]
    t = [x + mm(x, p) for x, p in zip(x1, p2)]
    sh = base
    while (1 << sh) < DN_CHUNK:
        merge = ((ri >> (sh + 1)) == (ci >> (sh + 1))) & ((ri >> sh) != (ci >> sh))
        y = [mm(jnp.where(merge, l, 0.0), x) for l, x in zip(lowers, t)]
        t = [x - mm(x, yy) for x, yy in zip(t, y)]
        sh += 1
    resid = [(eye - x) - _split_dot(l, x) for l, x in zip(lowers, t)]
    return [x + mm(x, r) for x, r in zip(t, resid)]


def _stack_heads(x, r0, c0, width):
    return jnp.concatenate(
        [x[r0:r0 + DN_CHUNK, c0 + h * width:c0 + (h + 1) * width] for h in range(DN_HEADS)], axis=0)


def _deltanet_kernel(qkv_ref, prev_ref, z_ref, ab_ref, convw_ref, alog_ref, dtb_ref, dng_ref,
                     ya_ref, state_ref):
    s = pl.program_id(1)
    ts = qkv_ref.shape[0]
    C = DN_CHUNK
    hd = DN_HEAD_DIM
    H = DN_HEADS
    HC = H * C

    @pl.when(s == 0)
    def _():
        state_ref[...] = jnp.zeros_like(state_ref)

    tile = qkv_ref[...]
    prev = jnp.where(s == 0, 0.0, prev_ref[...])
    xf = jnp.concatenate([prev, tile], axis=0)
    cw = convw_ref[...]
    acc = tile * cw[DN_CONV - 1:DN_CONV]
    for sft in range(1, DN_CONV):
        shifted = pltpu.roll(xf, sft, 0)[SUBLANES:SUBLANES + ts]
        acc = acc + shifted * cw[DN_CONV - 1 - sft:DN_CONV - sft]
    act = _silu(acc)

    ab = ab_ref[...]
    neg_a = -jnp.exp(alog_ref[...])
    xs = ab + dtb_ref[...]
    softplus = jnp.maximum(xs, 0.0) + jnp.log1p(jnp.exp(-jnp.abs(xs)))
    g_all = neg_a * softplus
    beta_all = _sigmoid(ab)

    zg = _silu(z_ref[...])
    dng = dng_ref[...]

    ri = lax.broadcasted_iota(jnp.int32, (HC, HC), 0)
    ci = lax.broadcasted_iota(jnp.int32, (HC, HC), 1)
    log2_c = C.bit_length() - 1
    same_head = (ri >> log2_c) == (ci >> log2_c)
    causal = same_head & (ri >= ci)
    strict = same_head & (ri > ci)
    eye = jnp.where(ri == ci, 1.0, 0.0).astype(F32)
    r64 = lax.broadcasted_iota(jnp.int32, (C, C), 0)
    c64 = lax.broadcasted_iota(jnp.int32, (C, C), 1)
    tril = jnp.where(r64 >= c64, 1.0, 0.0).astype(F32)

    chunks = range(ts // C)
    gcum = [_dot_f32(tril, g_all[c * C:(c + 1) * C]) for c in chunks]
    gc_col = [_stack_heads(g, 0, 0, 1) for g in gcum]
    gc_row = [jnp.concatenate([g.T[h:h + 1, :] for h in range(H)], axis=1) for g in gcum]
    g_last = [jnp.concatenate([jnp.broadcast_to(g[C - 1:C, h:h + 1], (C, 1)) for h in range(H)], axis=0)
              for g in gcum]
    beta = [_stack_heads(beta_all, c * C, H, 1) for c in chunks]
    q = [_stack_heads(act, c * C, 0, hd) for c in chunks]
    k = [_stack_heads(act, c * C, DN_WIDTH, hd) for c in chunks]
    v = [_stack_heads(act, c * C, 2 * DN_WIDTH, hd) for c in chunks]
    q = [x * lax.rsqrt(jnp.sum(x * x, axis=-1, keepdims=True) + EPS) * (hd ** -0.5) for x in q]
    k = [x * lax.rsqrt(jnp.sum(x * x, axis=-1, keepdims=True) + EPS) for x in k]
    decay = [jnp.where(causal, jnp.exp(jnp.where(causal, gc - gr, 0.0)), 0.0)
             for gc, gr in zip(gc_col, gc_row)]
    eg = [jnp.exp(gc) for gc in gc_col]
    kb = [x * b for x, b in zip(k, beta)]
    k16 = [x.astype(BF16) for x in k]
    lower = [jnp.where(strict, _dot_nt(a.astype(BF16), b) * d, 0.0) for a, b, d in zip(kb, k16, decay)]
    t16 = [t.astype(BF16) for t in _unit_lower_inverses(lower, eye, ri, ci)]
    uw = [_dot(t, jnp.concatenate([x * b, y * e], axis=1).astype(BF16))
          for t, x, b, y, e in zip(t16, v, beta, kb, eg)]
    attn = [(_dot_nt(x.astype(BF16), y) * d).astype(BF16) for x, y, d in zip(q, k16, decay)]
    per_chunk = [dict(u=uw[c][:, :hd], w=uw[c][:, hd:].astype(BF16), attn=attn[c],
                      qd=(q[c] * eg[c]).astype(BF16),
                      kd=(k[c] * jnp.exp(g_last[c] - gc_col[c])).astype(BF16),
                      el=[jnp.exp(gcum[c][C - 1:C, h:h + 1]) for h in range(H)]) for c in chunks]

    for c, pc in enumerate(per_chunk):
        r0 = c * C
        v_new = []
        q_state = []
        for h in range(H):
            rows = slice(h * C, (h + 1) * C)
            st16 = state_ref[h].astype(BF16)
            both = _dot(jnp.concatenate([pc["w"][rows], pc["qd"][rows]], axis=0), st16)
            v_new.append(pc["u"][rows] - both[:C])
            q_state.append(both[C:])
        vn16 = jnp.concatenate(v_new, axis=0).astype(BF16)
        intra = _dot(pc["attn"], vn16)
        for h in range(H):
            rows = slice(h * C, (h + 1) * C)
            state_ref[h] = state_ref[h] * pc["el"][h] + _dot_tn(pc["kd"][rows], vn16[rows])
            o = q_state[h] + intra[rows]
            y = _rms_scale(o) * dng * zg[r0:r0 + C, h * hd:(h + 1) * hd]
            ya_ref[r0:r0 + C, h * hd:(h + 1) * hd] = y.astype(ya_ref.dtype)


def _deltanet(qkv, z, ab, conv_w, a_log, dt_bias, dn_norm_g, batch, seq, ts):
    N = qkv.shape[0]
    per_seq = seq // ts
    n_qkv = qkv.shape[1]
    pad = LANES - DN_HEADS
    alog_row = jnp.pad(a_log, (0, pad)).reshape(1, LANES)
    dtb_row = jnp.pad(dt_bias, (0, pad)).reshape(1, LANES)

    def row(b, s):
        return (b * per_seq + s, 0)

    def prev(b, s):
        return (jnp.maximum((b * per_seq + s) * (ts // SUBLANES) - 1, 0), 0)

    def const(b, s):
        return (0, 0)

    return pl.pallas_call(
        _deltanet_kernel,
        out_shape=jax.ShapeDtypeStruct((N, DN_WIDTH), BF16),
        grid=(batch, per_seq),
        in_specs=[pl.BlockSpec((ts, n_qkv), row),
                  pl.BlockSpec((SUBLANES, n_qkv), prev),
                  pl.BlockSpec((ts, DN_WIDTH), row),
                  pl.BlockSpec((ts, LANES), row),
                  pl.BlockSpec(conv_w.shape, const),
                  pl.BlockSpec((1, LANES), const),
                  pl.BlockSpec((1, LANES), const),
                  pl.BlockSpec((1, DN_HEAD_DIM), const)],
        out_specs=pl.BlockSpec((ts, DN_WIDTH), row),
        scratch_shapes=[pltpu.VMEM((DN_HEADS, DN_HEAD_DIM, DN_HEAD_DIM), F32)],
        compiler_params=pltpu.CompilerParams(dimension_semantics=("arbitrary", "arbitrary"),
                                             vmem_limit_bytes=VMEM_LIMIT),
        name="deltanet",
    )(qkv, qkv, z, ab, conv_w, alog_row, dtb_row, dn_norm_g.reshape(1, DN_HEAD_DIM))


def _mix_kernel(ya_ref, pu_ref, pprev_ref, ga_ref, gb_ref, x_ref, gt1_ref, sc2_ref, sh2_ref, g2_ref,
                poolw_ref, pscale_ref, wla_ref, wlb_ref, wout_ref, wr_ref, br_ref,
                h_ref, xn_ref, route_ref, counts_ref, carry_ref, *, per_seq):
    i = pl.program_id(0)
    tm = x_ref.shape[0]
    first = (i % per_seq) == 0

    @pl.when(i == 0)
    def _():
        carry_ref[...] = jnp.zeros_like(carry_ref)

    pu = pu_ref[...]
    hist = jnp.where(first, 0.0, pprev_ref[...])
    pf = jnp.concatenate([hist, pu], axis=0)
    pos = (i % per_seq) * tm + lax.broadcasted_iota(jnp.int32, (tm, 1), 0)
    pscale = pscale_ref[...]
    yb_parts = []
    for gi, win in enumerate(POOL_WINDOWS):
        lo = gi * POOL_GROUP_DIM
        xg = pf[:, lo:lo + POOL_GROUP_DIM]
        acc = xg
        span = 1
        while span < win:
            acc = acc + pltpu.roll(acc, span, 0)
            span *= 2
        cnt = jnp.minimum(pos + 1, win).astype(F32)
        pooled = acc[POOL_HIST:] / cnt - pu[:, lo:lo + POOL_GROUP_DIM]
        yg = _dot(pooled.astype(BF16), poolw_ref[gi])
        yb_parts.append(yg * pscale[:, lo:lo + POOL_GROUP_DIM])
    yb = jnp.concatenate(yb_parts, axis=1)

    lift_a = _dot(ya_ref[...], wla_ref[...])
    lift_b = _dot(yb.astype(BF16), wlb_ref[...])
    mixed = _sigmoid(ga_ref[...]) * lift_a + _sigmoid(gb_ref[...]) * lift_b
    h = x_ref[...] + gt1_ref[0] * _dot(mixed.astype(BF16), wout_ref[...])
    h_ref[...] = h
    xn = _rms_scale(h) * g2_ref[...] * (1.0 + sc2_ref[0]) + sh2_ref[0]
    xn_ref[...] = xn

    logits = _dot_f32(xn, wr_ref[...]) + br_ref[...]
    li = lax.broadcasted_iota(jnp.int32, (tm, LANES), 1).astype(F32)
    neg = -jnp.inf
    big = float(LANES)
    is_g = li < MOE_GROUPS
    lg = jnp.where(is_g, logits, neg)
    mg = jnp.max(lg, axis=-1, keepdims=True)
    p_grp = 1.0 / jnp.sum(jnp.where(is_g, jnp.exp(lg - mg), 0.0), axis=-1, keepdims=True)
    grp = jnp.min(jnp.where(lg == mg, li, big), axis=-1, keepdims=True)
    e_lo = MOE_GROUPS + MOE_EXPERTS_PER_GROUP * grp
    is_e = (li >= e_lo) & (li < e_lo + MOE_EXPERTS_PER_GROUP)
    le = jnp.where(is_e, logits, neg)
    m1 = jnp.max(le, axis=-1, keepdims=True)
    i1 = jnp.min(jnp.where(le == m1, li, big), axis=-1, keepdims=True)
    le2 = jnp.where(li == i1, neg, le)
    m2 = jnp.max(le2, axis=-1, keepdims=True)
    i2 = jnp.min(jnp.where(le2 == m2, li, big), axis=-1, keepdims=True)
    e2 = jnp.exp(m2 - m1)
    w1 = p_grp / (1.0 + e2)
    w2 = p_grp * e2 / (1.0 + e2)
    eid1 = i1 - MOE_GROUPS
    eid2 = i2 - MOE_GROUPS

    oh1 = li == eid1
    oh2 = li == eid2
    onehot = jnp.where(oh1 | oh2, 1.0, 0.0)
    ri = lax.broadcasted_iota(jnp.int32, (tm, tm), 0)
    ci = lax.broadcasted_iota(jnp.int32, (tm, tm), 1)
    below = jnp.where(ri > ci, 1.0, 0.0).astype(BF16)
    carry = carry_ref[0:1, :]
    before = _dot(below, onehot.astype(BF16)) + carry
    rank1 = jnp.sum(jnp.where(oh1, before, 0.0), axis=-1, keepdims=True)
    rank2 = jnp.sum(jnp.where(oh2, before, 0.0), axis=-1, keepdims=True)
    new_carry = carry + jnp.sum(onehot, axis=0, keepdims=True)
    carry_ref[...] = jnp.broadcast_to(new_carry, carry_ref.shape)
    counts_ref[...] = jnp.broadcast_to(new_carry, counts_ref.shape)

    route = jnp.where(li == 0.0, w1, 0.0)
    for lane, val in ((1, w2), (2, eid1), (3, eid2), (4, rank1), (5, rank2)):
        route = jnp.where(li == float(lane), val, route)
    route_ref[...] = route


def _mix(ya, pu, ga, gb, x2, gt1, sc2, sh2, g2, pool_w, pool_scale, w_lift_a, w_lift_b, w_out,
         w_router, b_router, seq, tm):
    N, D = x2.shape
    per_seq = seq // tm

    def row(i):
        return (i, 0)

    def prev(i):
        return (jnp.maximum(i * (tm // POOL_HIST) - 1, 0), 0)

    def mod(i):
        return (i // per_seq, 0, 0)

    def const(i):
        return (0, 0)

    return pl.pallas_call(
        functools.partial(_mix_kernel, per_seq=per_seq),
        out_shape=[jax.ShapeDtypeStruct((N, D), F32),
                   jax.ShapeDtypeStruct((N, D), F32),
                   jax.ShapeDtypeStruct((N, LANES), F32),
                   jax.ShapeDtypeStruct((SUBLANES, LANES), F32)],
        grid=(N // tm,),
        in_specs=[pl.BlockSpec((tm, DN_WIDTH), row),
                  pl.BlockSpec((tm, POOL_WIDTH), row),
                  pl.BlockSpec((POOL_HIST, POOL_WIDTH), prev),
                  pl.BlockSpec((tm, D), row),
                  pl.BlockSpec((tm, D), row),
                  pl.BlockSpec((tm, D), row),
                  pl.BlockSpec((1, 1, D), mod),
                  pl.BlockSpec((1, 1, D), mod),
                  pl.BlockSpec((1, 1, D), mod),
                  pl.BlockSpec((1, D), const),
                  pl.BlockSpec(pool_w.shape, lambda i: (0, 0, 0)),
                  pl.BlockSpec((1, POOL_WIDTH), const),
                  pl.BlockSpec(w_lift_a.shape, const),
                  pl.BlockSpec(w_lift_b.shape, const),
                  pl.BlockSpec(w_out.shape, const),
                  pl.BlockSpec(w_router.shape, const),
                  pl.BlockSpec((1, LANES), const)],
        out_specs=[pl.BlockSpec((tm, D), row),
                   pl.BlockSpec((tm, D), row),
                   pl.BlockSpec((tm, LANES), row),
                   pl.BlockSpec((SUBLANES, LANES), const)],
        scratch_shapes=[pltpu.VMEM((SUBLANES, LANES), F32)],
        compiler_params=pltpu.CompilerParams(dimension_semantics=("arbitrary",),
                                             vmem_limit_bytes=VMEM_LIMIT),
        name="mix",
    )(ya, pu, pu, ga, gb, x2, gt1, sc2, sh2, g2, pool_w, pool_scale, w_lift_a, w_lift_b, w_out,
      w_router, b_router)


def _dest_kernel(route_ref, pstart_ref, dest_ref):
    route = route_ref[...]
    tm = route.shape[0]
    li = lax.broadcasted_iota(jnp.int32, (tm, LANES), 1).astype(F32)
    pstart = pstart_ref[...]
    out = jnp.zeros((tm, LANES), F32)
    for k in range(2):
        eid = route[:, 2 + k:3 + k]
        rank = route[:, 4 + k:5 + k]
        slot = jnp.sum(jnp.where(li == eid, pstart, 0.0), axis=-1, keepdims=True) + rank
        out = jnp.where(li == float(k), slot, out)
    dest_ref[...] = out.astype(jnp.int32)


def _dest(route, pstart_row, tm):
    N = route.shape[0]
    return pl.pallas_call(
        _dest_kernel,
        out_shape=jax.ShapeDtypeStruct((N, LANES), jnp.int32),
        grid=(N // tm,),
        in_specs=[pl.BlockSpec((tm, LANES), lambda i: (i, 0)),
                  pl.BlockSpec((1, LANES), lambda i: (0, 0))],
        out_specs=pl.BlockSpec((tm, LANES), lambda i: (i, 0)),
        compiler_params=pltpu.CompilerParams(dimension_semantics=("arbitrary",)),
        name="dest",
    )(route, pstart_row)


def _dispatch_kernel(d0_ref, d1_ref, x_ref, xs_in_ref, xs_ref, sem):
    del xs_in_ref
    tm = x_ref.shape[0]

    def row_copy(t, d):
        return pltpu.make_async_copy(x_ref.at[pl.ds(t, 1)], xs_ref.at[pl.ds(d, 1)], sem)

    def issue(t, carry):
        row_copy(t, d0_ref[0, t]).start()
        row_copy(t, d1_ref[0, t]).start()
        return carry

    lax.fori_loop(0, tm, issue, 0)

    def drain(t, carry):
        row_copy(t, d0_ref[0, t]).wait()
        row_copy(t, d1_ref[0, t]).wait()
        return carry

    lax.fori_loop(0, tm, drain, 0)


def _dispatch(d0, d1, xn, n_slots, tm):
    N, D = xn.shape
    nt = N // tm
    xs0 = jnp.zeros((n_slots, D), xn.dtype)
    smem = functools.partial(pl.BlockSpec, (None, 1, tm), lambda i: (i, 0, 0),
                             memory_space=pltpu.SMEM)
    return pl.pallas_call(
        _dispatch_kernel,
        out_shape=jax.ShapeDtypeStruct((n_slots, D), xn.dtype),
        grid=(nt,),
        in_specs=[smem(), smem(),
                  pl.BlockSpec((tm, D), lambda i: (i, 0)),
                  pl.BlockSpec(memory_space=pl.ANY)],
        out_specs=pl.BlockSpec(memory_space=pl.ANY),
        scratch_shapes=[pltpu.SemaphoreType.DMA],
        input_output_aliases={3: 0},
        compiler_params=pltpu.CompilerParams(dimension_semantics=("arbitrary",),
                                             vmem_limit_bytes=VMEM_LIMIT),
        name="dispatch",
    )(d0.reshape(nt, 1, tm), d1.reshape(nt, 1, tm), xn, xs0)


def _experts_kernel(blk_e_ref, nused_ref, x_ref, wg_ref, wu_ref, wd_ref, y_ref):
    del blk_e_ref
    used = pl.program_id(0) < nused_ref[0]

    @pl.when(used)
    def _():
        x = x_ref[...].astype(BF16)
        hid = _silu(_dot(x, wg_ref[...])) * _dot(x, wu_ref[...])
        y_ref[...] = _dot(hid.astype(BF16), wd_ref[...])

    @pl.when(jnp.logical_not(used))
    def _():
        y_ref[...] = jnp.zeros_like(y_ref)


def _experts(blk_e, nused, xs, w_gate, w_up, w_down):
    P, D = xs.shape
    nb = P // MOE_ROWS
    dff = w_gate.shape[2]

    def rows(j, blk_e_ref, nused_ref):
        return (jnp.minimum(j, nused_ref[0] - 1), 0)

    def wsel(j, blk_e_ref, nused_ref):
        return (blk_e_ref[j], 0, 0)

    return pl.pallas_call(
        _experts_kernel,
        out_shape=jax.ShapeDtypeStruct((P, D), F32),
        grid_spec=pltpu.PrefetchScalarGridSpec(
            num_scalar_prefetch=2,
            grid=(nb,),
            in_specs=[pl.BlockSpec((MOE_ROWS, D), rows),
                      pl.BlockSpec((None, D, dff), wsel),
                      pl.BlockSpec((None, D, dff), wsel),
                      pl.BlockSpec((None, dff, D), wsel)],
            out_specs=pl.BlockSpec((MOE_ROWS, D), lambda j, blk_e_ref, nused_ref: (j, 0))),
        compiler_params=pltpu.CompilerParams(dimension_semantics=("arbitrary",),
                                             vmem_limit_bytes=VMEM_LIMIT),
        name="experts",
    )(blk_e, nused, xs, w_gate, w_up, w_down)


def _combine_kernel(d0_ref, d1_ref, route_ref, h_ref, gt2_ref, gf_ref, y_ref, o_ref,
                    buf0_ref, buf1_ref, sem):
    tm = h_ref.shape[0]

    def row_copy(buf_ref, t, d):
        return pltpu.make_async_copy(y_ref.at[pl.ds(d, 1)], buf_ref.at[pl.ds(t, 1)], sem)

    def issue(t, carry):
        row_copy(buf0_ref, t, d0_ref[0, t]).start()
        row_copy(buf1_ref, t, d1_ref[0, t]).start()
        return carry

    lax.fori_loop(0, tm, issue, 0)

    def drain(t, carry):
        row_copy(buf0_ref, t, d0_ref[0, t]).wait()
        row_copy(buf1_ref, t, d1_ref[0, t]).wait()
        return carry

    lax.fori_loop(0, tm, drain, 0)

    route = route_ref[...]
    moe = buf0_ref[...] * route[:, 0:1] + buf1_ref[...] * route[:, 1:2]
    h = h_ref[...] + gt2_ref[0] * moe
    o_ref[...] = _rms_scale(h) * gf_ref[...]


def _combine(d0, d1, route, h, gt2, gf, y, seq, tm):
    N, D = h.shape
    nt = N // tm
    per_seq = seq // tm
    smem = functools.partial(pl.BlockSpec, (None, 1, tm), lambda i: (i, 0, 0),
                             memory_space=pltpu.SMEM)
    return pl.pallas_call(
        _combine_kernel,
        out_shape=jax.ShapeDtypeStruct((N, D), F32),
        grid=(nt,),
        in_specs=[smem(), smem(),
                  pl.BlockSpec((tm, LANES), lambda i: (i, 0)),
                  pl.BlockSpec((tm, D), lambda i: (i, 0)),
                  pl.BlockSpec((1, 1, D), lambda i: (i // per_seq, 0, 0)),
                  pl.BlockSpec((1, D), lambda i: (0, 0)),
                  pl.BlockSpec(memory_space=pl.ANY)],
        out_specs=pl.BlockSpec((tm, D), lambda i: (i, 0)),
        scratch_shapes=[pltpu.VMEM((tm, D), F32), pltpu.VMEM((tm, D), F32), pltpu.SemaphoreType.DMA],
        compiler_params=pltpu.CompilerParams(dimension_semantics=("arbitrary",),
                                             vmem_limit_bytes=VMEM_LIMIT),
        name="combine",
    )(d0.reshape(nt, 1, tm), d1.reshape(nt, 1, tm), route, h, gt2, gf, y)


def _layer(h2, mod, batch, seq, norm1_g, w_in, conv_w, a_log, dt_bias, dn_norm_g, pool_w, pool_scale,
           w_lift_a, w_lift_b, w_out, norm2_g, w_rg, b_rg, w_re, b_re, w_gate, w_up, w_down, final_g):
    N, D = h2.shape
    sh1, sc1, gt1, sh2, sc2, gt2 = [m.reshape(batch, 1, D) for m in jnp.split(mod, 6, axis=-1)]

    o_z = 3 * DN_WIDTH
    o_a = o_z + DN_WIDTH
    o_pu = o_a + 2 * DN_HEADS
    o_g = o_pu + POOL_WIDTH
    w_qkvz = w_in[:, :o_a].astype(BF16)
    w_ab = jnp.pad(w_in[:, o_a:o_pu], ((0, 0), (0, LANES - 2 * DN_HEADS))).astype(BF16)
    w_pu = w_in[:, o_pu:o_g].astype(BF16)
    w_g = w_in[:, o_g:].astype(BF16)

    tm = min(256, seq)
    qkv, z, ab, pu, ga, gb = _inproj(h2, norm1_g.reshape(1, D), sc1, sh1, w_qkvz, w_ab, w_pu, w_g, seq, tm)

    ts = min(256, seq)
    ya = _deltanet(qkv, z, ab, conv_w, a_log, dt_bias, dn_norm_g, batch, seq, ts)

    n_r = MOE_GROUPS + MOE_EXPERTS
    w_router = jnp.pad(jnp.concatenate([w_rg, w_re], axis=1), ((0, 0), (0, LANES - n_r)))
    b_router = jnp.pad(jnp.concatenate([b_rg, b_re]), (0, LANES - n_r)).reshape(1, LANES)
    h_mid, xn2, route, counts = _mix(
        ya, pu, ga, gb, h2, gt1, sc2, sh2, norm2_g.reshape(1, D), pool_w.astype(BF16),
        pool_scale.reshape(1, POOL_WIDTH), w_lift_a.astype(BF16), w_lift_b.astype(BF16),
        w_out.astype(BF16), w_router, b_router, seq, tm)

    cnt = counts[0, :MOE_EXPERTS].astype(jnp.int32)
    padded = (cnt + MOE_ROWS - 1) // MOE_ROWS * MOE_ROWS
    pend = jnp.cumsum(padded)
    pstart = pend - padded
    n_slots = 2 * N + MOE_EXPERTS * MOE_ROWS
    nb = n_slots // MOE_ROWS
    nused = (pend[-1] // MOE_ROWS).astype(jnp.int32).reshape(1)
    blk_row0 = jnp.arange(nb, dtype=jnp.int32) * MOE_ROWS
    blk_e = jnp.minimum(jnp.sum((pend[None, :] <= blk_row0[:, None]).astype(jnp.int32), axis=1),
                        MOE_EXPERTS - 1)
    pstart_row = jnp.pad(pstart.astype(F32), (0, LANES - MOE_EXPERTS)).reshape(1, LANES)

    dest = _dest(route, pstart_row, tm)
    d0 = dest[:, 0]
    d1 = dest[:, 1]
    xs = _dispatch(d0, d1, xn2, n_slots, tm)
    y = _experts(blk_e, nused, xs, w_gate.astype(BF16), w_up.astype(BF16), w_down.astype(BF16))
    return _combine(d0, d1, route, h_mid, gt2, final_g.reshape(1, D), y, seq, tm)


def kernel(x, c, w_ada, b_ada, norm1_g, w_in, conv_w, a_log, dt_bias, dn_norm_g, pool_w, pool_scale,
           w_lift_a, w_lift_b, w_out, norm2_g, w_router_group, b_router_group, w_router_expert,
           b_router_expert, w_gate, w_up, w_down, final_norm_g):
    batch, seq, D = x.shape
    depth = w_ada.shape[0]
    assert depth == 1, "the combine kernel fuses the final rmsnorm, so exactly one layer is supported"
    assert seq % DN_CHUNK == 0 and D % LANES == 0
    h2 = x.reshape(batch * seq, D)
    l = 0
    mod = _ada(c, w_ada[l], b_ada[l])
    out = _layer(h2, mod, batch, seq, norm1_g[l], w_in[l], conv_w[l], a_log[l], dt_bias[l], dn_norm_g[l],
                 pool_w[l], pool_scale[l], w_lift_a[l], w_lift_b[l], w_out[l], norm2_g[l],
                 w_router_group[l], b_router_group[l], w_router_expert[l], b_router_expert[l],
                 w_gate[l], w_up[l], w_down[l], final_norm_g)
    return out.reshape(batch, seq, D)
```

```python
import functools

import jax
import jax.numpy as jnp
from jax import lax
from jax.experimental import pallas as pl
from jax.experimental.pallas import tpu as pltpu

F32 = jnp.float32
BF16 = jnp.bfloat16
HIGHEST = lax.Precision.HIGHEST

EPS = 1e-6
LANES = 128
SUBLANES = 8
DN_HEADS = 4
DN_HEAD_DIM = 128
DN_WIDTH = DN_HEADS * DN_HEAD_DIM
DN_CONV = 4
DN_CHUNK = 64
POOL_WINDOWS = (2, 4, 8, 16)
POOL_GROUP_DIM = 128
POOL_WIDTH = POOL_GROUP_DIM * len(POOL_WINDOWS)
POOL_HIST = 16
MOE_GROUPS = 4
MOE_EXPERTS_PER_GROUP = 8
MOE_EXPERTS = MOE_GROUPS * MOE_EXPERTS_PER_GROUP
MOE_ROWS = 256
VMEM_LIMIT = 56 * 1024 * 1024


def _sigmoid(x):
    return 1.0 / (1.0 + jnp.exp(-x))


def _silu(x):
    return x * _sigmoid(x)


def _dot(a, b):
    return jnp.dot(a, b, preferred_element_type=F32)


def _dot_f32(a, b):
    return jnp.dot(a, b, preferred_element_type=F32, precision=HIGHEST)


def _dot_nt(a, b):
    return lax.dot_general(a, b, (((1,), (1,)), ((), ())), preferred_element_type=F32)


def _dot_tn(a, b):
    return lax.dot_general(a, b, (((0,), (0,)), ((), ())), preferred_element_type=F32)


def _split_dot(a, b):
    a_hi = a.astype(BF16)
    a_lo = (a - a_hi.astype(F32)).astype(BF16)
    b_hi = b.astype(BF16)
    b_lo = (b - b_hi.astype(F32)).astype(BF16)
    return _dot(a_hi, b_hi) + (_dot(a_hi, b_lo) + _dot(a_lo, b_hi))


def _rms_scale(x):
    return x * lax.rsqrt(jnp.mean(x * x, axis=-1, keepdims=True) + EPS)


def _ada_kernel(c_ref, w_ref, b_ref, o_ref):
    o_ref[...] = _dot_f32(_silu(c_ref[...]), w_ref[...]) + b_ref[...]


def _ada(c, w_ada, b_ada):
    B, D = c.shape
    n_out = w_ada.shape[1]
    tn = D
    return pl.pallas_call(
        _ada_kernel,
        out_shape=jax.ShapeDtypeStruct((B, n_out), F32),
        grid=(n_out // tn,),
        in_specs=[pl.BlockSpec((B, D), lambda j: (0, 0)),
                  pl.BlockSpec((D, tn), lambda j: (0, j)),
                  pl.BlockSpec((1, tn), lambda j: (0, j))],
        out_specs=pl.BlockSpec((B, tn), lambda j: (0, j)),
        compiler_params=pltpu.CompilerParams(dimension_semantics=("arbitrary",),
                                             vmem_limit_bytes=VMEM_LIMIT),
        name="ada",
    )(c, w_ada, b_ada.reshape(1, n_out))


def _inproj_kernel(x_ref, g_ref, sc_ref, sh_ref, wqkvz_ref, wab_ref, wpu_ref, wg_ref,
                   qkv_ref, z_ref, ab_ref, pu_ref, ga_ref, gb_ref):
    x = x_ref[...]
    xn = _rms_scale(x) * g_ref[...] * (1.0 + sc_ref[0]) + sh_ref[0]
    xb = xn.astype(BF16)
    n_qkv = qkv_ref.shape[1]
    d = ga_ref.shape[1]
    qkvz = _dot(xb, wqkvz_ref[...])
    qkv_ref[...] = qkvz[:, :n_qkv]
    z_ref[...] = qkvz[:, n_qkv:]
    ab_ref[...] = _dot(xb, wab_ref[...])
    pu_ref[...] = _dot(xb, wpu_ref[...])
    gates = _dot(xb, wg_ref[...])
    ga_ref[...] = gates[:, :d]
    gb_ref[...] = gates[:, d:]


def _inproj(x2, g, sc, sh, w_qkvz, w_ab, w_pu, w_g, seq, tm):
    N, D = x2.shape
    per_seq = seq // tm
    n_qkv = 3 * DN_WIDTH

    def row(i):
        return (i, 0)

    def mod(i):
        return (i // per_seq, 0, 0)

    def const(i):
        return (0, 0)

    outs = [(N, n_qkv), (N, DN_WIDTH), (N, LANES), (N, POOL_WIDTH), (N, D), (N, D)]
    return pl.pallas_call(
        _inproj_kernel,
        out_shape=[jax.ShapeDtypeStruct(s, F32) for s in outs],
        grid=(N // tm,),
        in_specs=[pl.BlockSpec((tm, D), row),
                  pl.BlockSpec((1, D), const),
                  pl.BlockSpec((1, 1, D), mod),
                  pl.BlockSpec((1, 1, D), mod),
                  pl.BlockSpec(w_qkvz.shape, const),
                  pl.BlockSpec(w_ab.shape, const),
                  pl.BlockSpec(w_pu.shape, const),
                  pl.BlockSpec(w_g.shape, const)],
        out_specs=[pl.BlockSpec((tm, s[1]), row) for s in outs],
        compiler_params=pltpu.CompilerParams(dimension_semantics=("arbitrary",),
                                             vmem_limit_bytes=VMEM_LIMIT),
        name="inproj",
    )(x2, g, sc, sh, w_qkvz, w_ab, w_pu, w_g)


def _unit_lower_inverses(lowers, eye, ri, ci):
    def mm(a, b):
        return _dot(a.astype(BF16), b.astype(BF16))

    base = 3
    same = (ri >> base) == (ci >> base)
    ld = [jnp.where(same, l, 0.0) for l in lowers]
    a0 = [eye - x for x in ld]
    p1 = [mm(x, x) for x in ld]
    x1 = [a + mm(a, p) for a, p in zip(a0, p1)]
    p2 = [mm(p, p) for p in p1]
    t = [x + mm(x, p) for x, p in zip(x1, p2)]
    sh = base
    while (1 << sh) < DN_CHUNK:
        merge = ((ri >> (sh + 1)) == (ci >> (sh + 1))) & ((ri >> sh) != (ci >> sh))
        y = [mm(jnp.where(merge, l, 0.0), x) for l, x in zip(lowers, t)]
        t = [x - mm(x, yy) for x, yy in zip(t, y)]
        sh += 1
    resid = [(eye - x) - _split_dot(l, x) for l, x in zip(lowers, t)]
    return [x + mm(x, r) for x, r in zip(t, resid)]


def _stack_heads(x, r0, c0, width):
    return jnp.concatenate(
        [x[r0:r0 + DN_CHUNK, c0 + h * width:c0 + (h + 1) * width] for h in range(DN_HEADS)], axis=0)


def _deltanet_kernel(qkv_ref, prev_ref, z_ref, ab_ref, convw_ref, alog_ref, dtb_ref, dng_ref,
                     ya_ref, state_ref):
    s = pl.program_id(1)
    ts = qkv_ref.shape[0]
    C = DN_CHUNK
    hd = DN_HEAD_DIM
    H = DN_HEADS
    HC = H * C

    @pl.when(s == 0)
    def _():
        state_ref[...] = jnp.zeros_like(state_ref)

    tile = qkv_ref[...]
    prev = jnp.where(s == 0, 0.0, prev_ref[...])
    xf = jnp.concatenate([prev, tile], axis=0)
    cw = convw_ref[...]
    acc = tile * cw[DN_CONV - 1:DN_CONV]
    for sft in range(1, DN_CONV):
        shifted = pltpu.roll(xf, sft, 0)[SUBLANES:SUBLANES + ts]
        acc = acc + shifted * cw[DN_CONV - 1 - sft:DN_CONV - sft]
    act = _silu(acc)

    ab = ab_ref[...]
    neg_a = -jnp.exp(alog_ref[...])
    xs = ab + dtb_ref[...]
    softplus = jnp.maximum(xs, 0.0) + jnp.log1p(jnp.exp(-jnp.abs(xs)))
    g_all = neg_a * softplus
    beta_all = _sigmoid(ab)

    zg = _silu(z_ref[...])
    dng = dng_ref[...]

    ri = lax.broadcasted_iota(jnp.int32, (HC, HC), 0)
    ci = lax.broadcasted_iota(jnp.int32, (HC, HC), 1)
    log2_c = C.bit_length() - 1
    same_head = (ri >> log2_c) == (ci >> log2_c)
    causal = same_head & (ri >= ci)
    strict = same_head & (ri > ci)
    eye = jnp.where(ri == ci, 1.0, 0.0).astype(F32)
    r64 = lax.broadcasted_iota(jnp.int32, (C, C), 0)
    c64 = lax.broadcasted_iota(jnp.int32, (C, C), 1)
    tril = jnp.where(r64 >= c64, 1.0, 0.0).astype(F32)

    chunks = range(ts // C)
    gcum = [_dot_f32(tril, g_all[c * C:(c + 1) * C]) for c in chunks]
    gc_col = [_stack_heads(g, 0, 0, 1) for g in gcum]
    gc_row = [jnp.concatenate([g.T[h:h + 1, :] for h in range(H)], axis=1) for g in gcum]
    g_last = [jnp.concatenate([jnp.broadcast_to(g[C - 1:C, h:h + 1], (C, 1)) for h in range(H)], axis=0)
              for g in gcum]
    beta = [_stack_heads(beta_all, c * C, H, 1) for c in chunks]
    q = [_stack_heads(act, c * C, 0, hd) for c in chunks]
    k = [_stack_heads(act, c * C, DN_WIDTH, hd) for c in chunks]
    v = [_stack_heads(act, c * C, 2 * DN_WIDTH, hd) for c in chunks]
    q = [x * lax.rsqrt(jnp.sum(x * x, axis=-1, keepdims=True) + EPS) * (hd ** -0.5) for x in q]
    k = [x * lax.rsqrt(jnp.sum(x * x, axis=-1, keepdims=True) + EPS) for x in k]
    decay = [jnp.where(causal, jnp.exp(jnp.where(causal, gc - gr, 0.0)), 0.0)
             for gc, gr in zip(gc_col, gc_row)]
    eg = [jnp.exp(gc) for gc in gc_col]
    kb = [x * b for x, b in zip(k, beta)]
    k16 = [x.astype(BF16) for x in k]
    lower = [jnp.where(strict, _dot_nt(a.astype(BF16), b) * d, 0.0) for a, b, d in zip(kb, k16, decay)]
    t16 = [t.astype(BF16) for t in _unit_lower_inverses(lower, eye, ri, ci)]
    uw = [_dot(t, jnp.concatenate([x * b, y * e], axis=1).astype(BF16))
          for t, x, b, y, e in zip(t16, v, beta, kb, eg)]
    attn = [(_dot_nt(x.astype(BF16), y) * d).astype(BF16) for x, y, d in zip(q, k16, decay)]
    per_chunk = [dict(u=uw[c][:, :hd], w=uw[c][:, hd:].astype(BF16), attn=attn[c],
                      qd=(q[c] * eg[c]).astype(BF16),
                      kd=(k[c] * jnp.exp(g_last[c] - gc_col[c])).astype(BF16),
                      el=[jnp.exp(gcum[c][C - 1:C, h:h + 1]) for h in range(H)]) for c in chunks]

    for c, pc in enumerate(per_chunk):
        r0 = c * C
        v_new = []
        q_state = []
        for h in range(H):
            rows = slice(h * C, (h + 1) * C)
            st16 = state_ref[h].astype(BF16)
            both = _dot(jnp.concatenate([pc["w"][rows], pc["qd"][rows]], axis=0), st16)
            v_new.append(pc["u"][rows] - both[:C])
            q_state.append(both[C:])
        vn16 = jnp.concatenate(v_new, axis=0).astype(BF16)
        intra = _dot(pc["attn"], vn16)
        for h in range(H):
            rows = slice(h * C, (h + 1) * C)
            state_ref[h] = state_ref[h] * pc["el"][h] + _dot_tn(pc["kd"][rows], vn16[rows])
            o = q_state[h] + intra[rows]
            y = _rms_scale(o) * dng * zg[r0:r0 + C, h * hd:(h + 1) * hd]
            ya_ref[r0:r0 + C, h * hd:(h + 1) * hd] = y.astype(ya_ref.dtype)


def _deltanet(qkv, z, ab, conv_w, a_log, dt_bias, dn_norm_g, batch, seq, ts):
    N = qkv.shape[0]
    per_seq = seq // ts
    n_qkv = qkv.shape[1]
    pad = LANES - DN_HEADS
    alog_row = jnp.pad(a_log, (0, pad)).reshape(1, LANES)
    dtb_row = jnp.pad(dt_bias, (0, pad)).reshape(1, LANES)

    def row(b, s):
        return (b * per_seq + s, 0)

    def prev(b, s):
        return (jnp.maximum((b * per_seq + s) * (ts // SUBLANES) - 1, 0), 0)

    def const(b, s):
        return (0, 0)

    return pl.pallas_call(
        _deltanet_kernel,
        out_shape=jax.ShapeDtypeStruct((N, DN_WIDTH), BF16),
        grid=(batch, per_seq),
        in_specs=[pl.BlockSpec((ts, n_qkv), row),
                  pl.BlockSpec((SUBLANES, n_qkv), prev),
                  pl.BlockSpec((ts, DN_WIDTH), row),
                  pl.BlockSpec((ts, LANES), row),
                  pl.BlockSpec(conv_w.shape, const),
                  pl.BlockSpec((1, LANES), const),
                  pl.BlockSpec((1, LANES), const),
                  pl.BlockSpec((1, DN_HEAD_DIM), const)],
        out_specs=pl.BlockSpec((ts, DN_WIDTH), row),
        scratch_shapes=[pltpu.VMEM((DN_HEADS, DN_HEAD_DIM, DN_HEAD_DIM), F32)],
        compiler_params=pltpu.CompilerParams(dimension_semantics=("arbitrary", "arbitrary"),
                                             vmem_limit_bytes=VMEM_LIMIT),
        name="deltanet",
    )(qkv, qkv, z, ab, conv_w, alog_row, dtb_row, dn_norm_g.reshape(1, DN_HEAD_DIM))


def _mix_kernel(ya_ref, pu_ref, pprev_ref, ga_ref, gb_ref, x_ref, gt1_ref, sc2_ref, sh2_ref, g2_ref,
                poolw_ref, pscale_ref, wla_ref, wlb_ref, wout_ref, wr_ref, br_ref,
                h_ref, xn_ref, route_ref, counts_ref, carry_ref, *, per_seq):
    i = pl.program_id(0)
    tm = x_ref.shape[0]
    first = (i % per_seq) == 0

    @pl.when(i == 0)
    def _():
        carry_ref[...] = jnp.zeros_like(carry_ref)

    pu = pu_ref[...]
    hist = jnp.where(first, 0.0, pprev_ref[...])
    pf = jnp.concatenate([hist, pu], axis=0)
    pos = (i % per_seq) * tm + lax.broadcasted_iota(jnp.int32, (tm, 1), 0)
    pscale = pscale_ref[...]
    yb_parts = []
    for gi, win in enumerate(POOL_WINDOWS):
        lo = gi * POOL_GROUP_DIM
        xg = pf[:, lo:lo + POOL_GROUP_DIM]
        acc = xg
        span = 1
        while span < win:
            acc = acc + pltpu.roll(acc, span, 0)
            span *= 2
        cnt = jnp.minimum(pos + 1, win).astype(F32)
        pooled = acc[POOL_HIST:] / cnt - pu[:, lo:lo + POOL_GROUP_DIM]
        yg = _dot(pooled.astype(BF16), poolw_ref[gi])
        yb_parts.append(yg * pscale[:, lo:lo + POOL_GROUP_DIM])
    yb = jnp.concatenate(yb_parts, axis=1)

    lift_a = _dot(ya_ref[...], wla_ref[...])
    lift_b = _dot(yb.astype(BF16), wlb_ref[...])
    mixed = _sigmoid(ga_ref[...]) * lift_a + _sigmoid(gb_ref[...]) * lift_b
    h = x_ref[...] + gt1_ref[0] * _dot(mixed.astype(BF16), wout_ref[...])
    h_ref[...] = h
    xn = _rms_scale(h) * g2_ref[...] * (1.0 + sc2_ref[0]) + sh2_ref[0]
    xn_ref[...] = xn

    xn_hi = xn.astype(BF16)
    xn_lo = (xn - xn_hi.astype(F32)).astype(BF16)
    logits = _dot(jnp.concatenate([xn_hi, xn_lo, xn_hi], axis=1), wr_ref[...]) + br_ref[...]
    li = lax.broadcasted_iota(jnp.int32, (tm, LANES), 1).astype(F32)
    neg = -jnp.inf
    big = float(LANES)
    is_g = li < MOE_GROUPS
    lg = jnp.where(is_g, logits, neg)
    mg = jnp.max(lg, axis=-1, keepdims=True)
    p_grp = 1.0 / jnp.sum(jnp.where(is_g, jnp.exp(lg - mg), 0.0), axis=-1, keepdims=True)
    grp = jnp.min(jnp.where(lg == mg, li, big), axis=-1, keepdims=True)
    e_lo = MOE_GROUPS + MOE_EXPERTS_PER_GROUP * grp
    is_e = (li >= e_lo) & (li < e_lo + MOE_EXPERTS_PER_GROUP)
    le = jnp.where(is_e, logits, neg)
    m1 = jnp.max(le, axis=-1, keepdims=True)
    i1 = jnp.min(jnp.where(le == m1, li, big), axis=-1, keepdims=True)
    le2 = jnp.where(li == i1, neg, le)
    m2 = jnp.max(le2, axis=-1, keepdims=True)
    i2 = jnp.min(jnp.where(le2 == m2, li, big), axis=-1, keepdims=True)
    e2 = jnp.exp(m2 - m1)
    w1 = p_grp / (1.0 + e2)
    w2 = p_grp * e2 / (1.0 + e2)
    eid1 = i1 - MOE_GROUPS
    eid2 = i2 - MOE_GROUPS

    oh1 = li == eid1
    oh2 = li == eid2
    onehot = jnp.where(oh1 | oh2, 1.0, 0.0)
    ri = lax.broadcasted_iota(jnp.int32, (tm, tm), 0)
    ci = lax.broadcasted_iota(jnp.int32, (tm, tm), 1)
    below = jnp.where(ri > ci, 1.0, 0.0).astype(BF16)
    carry = carry_ref[0:1, :]
    before = _dot(below, onehot.astype(BF16)) + carry
    rank1 = jnp.sum(jnp.where(oh1, before, 0.0), axis=-1, keepdims=True)
    rank2 = jnp.sum(jnp.where(oh2, before, 0.0), axis=-1, keepdims=True)
    new_carry = carry + jnp.sum(onehot, axis=0, keepdims=True)
    carry_ref[...] = jnp.broadcast_to(new_carry, carry_ref.shape)
    counts_ref[...] = jnp.broadcast_to(new_carry, counts_ref.shape)

    route = jnp.where(li == 0.0, w1, 0.0)
    for lane, val in ((1, w2), (2, eid1), (3, eid2), (4, rank1), (5, rank2)):
        route = jnp.where(li == float(lane), val, route)
    route_ref[...] = route


def _mix(ya, pu, ga, gb, x2, gt1, sc2, sh2, g2, pool_w, pool_scale, w_lift_a, w_lift_b, w_out,
         w_router, b_router, seq, tm):
    N, D = x2.shape
    per_seq = seq // tm

    def row(i):
        return (i, 0)

    def prev(i):
        return (jnp.maximum(i * (tm // POOL_HIST) - 1, 0), 0)

    def mod(i):
        return (i // per_seq, 0, 0)

    def const(i):
        return (0, 0)

    return pl.pallas_call(
        functools.partial(_mix_kernel, per_seq=per_seq),
        out_shape=[jax.ShapeDtypeStruct((N, D), F32),
                   jax.ShapeDtypeStruct((N, D), F32),
                   jax.ShapeDtypeStruct((N, LANES), F32),
                   jax.ShapeDtypeStruct((SUBLANES, LANES), F32)],
        grid=(N // tm,),
        in_specs=[pl.BlockSpec((tm, DN_WIDTH), row),
                  pl.BlockSpec((tm, POOL_WIDTH), row),
                  pl.BlockSpec((POOL_HIST, POOL_WIDTH), prev),
                  pl.BlockSpec((tm, D), row),
                  pl.BlockSpec((tm, D), row),
                  pl.BlockSpec((tm, D), row),
                  pl.BlockSpec((1, 1, D), mod),
                  pl.BlockSpec((1, 1, D), mod),
                  pl.BlockSpec((1, 1, D), mod),
                  pl.BlockSpec((1, D), const),
                  pl.BlockSpec(pool_w.shape, lambda i: (0, 0, 0)),
                  pl.BlockSpec((1, POOL_WIDTH), const),
                  pl.BlockSpec(w_lift_a.shape, const),
                  pl.BlockSpec(w_lift_b.shape, const),
                  pl.BlockSpec(w_out.shape, const),
                  pl.BlockSpec(w_router.shape, const),
                  pl.BlockSpec((1, LANES), const)],
        out_specs=[pl.BlockSpec((tm, D), row),
                   pl.BlockSpec((tm, D), row),
                   pl.BlockSpec((tm, LANES), row),
                   pl.BlockSpec((SUBLANES, LANES), const)],
        scratch_shapes=[pltpu.VMEM((SUBLANES, LANES), F32)],
        compiler_params=pltpu.CompilerParams(dimension_semantics=("arbitrary",),
                                             vmem_limit_bytes=VMEM_LIMIT),
        name="mix",
    )(ya, pu, pu, ga, gb, x2, gt1, sc2, sh2, g2, pool_w, pool_scale, w_lift_a, w_lift_b, w_out,
      w_router, b_router)


def _dest_kernel(route_ref, pstart_ref, dest_ref):
    route = route_ref[...]
    tm = route.shape[0]
    li = lax.broadcasted_iota(jnp.int32, (tm, LANES), 1).astype(F32)
    pstart = pstart_ref[...]
    out = jnp.zeros((tm, LANES), F32)
    for k in range(2):
        eid = route[:, 2 + k:3 + k]
        rank = route[:, 4 + k:5 + k]
        slot = jnp.sum(jnp.where(li == eid, pstart, 0.0), axis=-1, keepdims=True) + rank
        out = jnp.where(li == float(k), slot, out)
    dest_ref[...] = out.astype(jnp.int32)


def _dest(route, pstart_row, tm):
    N = route.shape[0]
    return pl.pallas_call(
        _dest_kernel,
        out_shape=jax.ShapeDtypeStruct((N, LANES), jnp.int32),
        grid=(N // tm,),
        in_specs=[pl.BlockSpec((tm, LANES), lambda i: (i, 0)),
                  pl.BlockSpec((1, LANES), lambda i: (0, 0))],
        out_specs=pl.BlockSpec((tm, LANES), lambda i: (i, 0)),
        compiler_params=pltpu.CompilerParams(dimension_semantics=("arbitrary",)),
        name="dest",
    )(route, pstart_row)


def _dispatch_kernel(pend_ref, padded_ref, nused_ref, d0_ref, d1_ref, x_ref, xs_ref, zero_ref, sem, zsem):
    tm = x_ref.shape[0]
    nb = xs_ref.shape[0] // MOE_ROWS

    @pl.when(pl.program_id(0) == 0)
    def _():
        zero_ref[...] = jnp.zeros_like(zero_ref)

        def zero_block(row0):
            return pltpu.make_async_copy(zero_ref, xs_ref.at[pl.ds(row0, MOE_ROWS)], zsem)

        def tail_block(e):
            return zero_block(pl.multiple_of(pend_ref[e] - MOE_ROWS, MOE_ROWS))

        def unused_block(j):
            return zero_block(pl.multiple_of(j * MOE_ROWS, MOE_ROWS))

        def for_each_zero_block(act):
            def per_expert(e, carry):
                @pl.when(padded_ref[e] > 0)
                def _():
                    act(tail_block(e))
                return carry

            def per_unused(j, carry):
                act(unused_block(j))
                return carry

            lax.fori_loop(0, MOE_EXPERTS, per_expert, 0)
            lax.fori_loop(nused_ref[0], nb, per_unused, 0)

        for_each_zero_block(lambda cp: cp.start())
        for_each_zero_block(lambda cp: cp.wait())

    def row_copy(t, d):
        return pltpu.make_async_copy(x_ref.at[pl.ds(t, 1)], xs_ref.at[pl.ds(d, 1)], sem)

    def issue(t, carry):
        row_copy(t, d0_ref[0, t]).start()
        row_copy(t, d1_ref[0, t]).start()
        return carry

    lax.fori_loop(0, tm, issue, 0, unroll=8)

    def drain(t, carry):
        row_copy(t, d0_ref[0, t]).wait()
        row_copy(t, d1_ref[0, t]).wait()
        return carry

    lax.fori_loop(0, tm, drain, 0, unroll=8)


def _dispatch(pend, padded, nused, d0, d1, xn, n_slots, tm):
    N, D = xn.shape
    nt = N // tm
    smem = functools.partial(pl.BlockSpec, (None, 1, tm), lambda i, *_: (i, 0, 0),
                             memory_space=pltpu.SMEM)
    return pl.pallas_call(
        _dispatch_kernel,
        out_shape=jax.ShapeDtypeStruct((n_slots, D), xn.dtype),
        grid_spec=pltpu.PrefetchScalarGridSpec(
            num_scalar_prefetch=3,
            grid=(nt,),
            in_specs=[smem(), smem(), pl.BlockSpec((tm, D), lambda i, *_: (i, 0))],
            out_specs=pl.BlockSpec(memory_space=pl.ANY),
            scratch_shapes=[pltpu.VMEM((MOE_ROWS, D), xn.dtype), pltpu.SemaphoreType.DMA,
                            pltpu.SemaphoreType.DMA]),
        compiler_params=pltpu.CompilerParams(dimension_semantics=("arbitrary",),
                                             vmem_limit_bytes=VMEM_LIMIT),
        name="dispatch",
    )(pend, padded, nused, d0.reshape(nt, 1, tm), d1.reshape(nt, 1, tm), xn)


def _experts_kernel(blk_e_ref, nused_ref, x_ref, wg_ref, wu_ref, wd_ref, y_ref):
    del blk_e_ref
    used = pl.program_id(0) < nused_ref[0]

    @pl.when(used)
    def _():
        x = x_ref[...].astype(BF16)
        hid = _silu(_dot(x, wg_ref[...])) * _dot(x, wu_ref[...])
        y_ref[...] = _dot(hid.astype(BF16), wd_ref[...])

    @pl.when(jnp.logical_not(used))
    def _():
        y_ref[...] = jnp.zeros_like(y_ref)


def _experts(blk_e, nused, xs, w_gate, w_up, w_down):
    P, D = xs.shape
    nb = P // MOE_ROWS
    dff = w_gate.shape[2]

    def rows(j, blk_e_ref, nused_ref):
        return (jnp.maximum(jnp.minimum(j, nused_ref[0] - 1), 0), 0)

    def wsel(j, blk_e_ref, nused_ref):
        return (blk_e_ref[j], 0, 0)

    return pl.pallas_call(
        _experts_kernel,
        out_shape=jax.ShapeDtypeStruct((P, D), F32),
        grid_spec=pltpu.PrefetchScalarGridSpec(
            num_scalar_prefetch=2,
            grid=(nb,),
            in_specs=[pl.BlockSpec((MOE_ROWS, D), rows),
                      pl.BlockSpec((None, D, dff), wsel),
                      pl.BlockSpec((None, D, dff), wsel),
                      pl.BlockSpec((None, dff, D), wsel)],
            out_specs=pl.BlockSpec((MOE_ROWS, D), lambda j, blk_e_ref, nused_ref: (j, 0))),
        compiler_params=pltpu.CompilerParams(dimension_semantics=("arbitrary",),
                                             vmem_limit_bytes=VMEM_LIMIT),
        name="experts",
    )(blk_e, nused, xs, w_gate, w_up, w_down)


def _combine_kernel(d0_ref, d1_ref, d0n_ref, d1n_ref, route_ref, h_ref, gt2_ref, gf_ref, y_ref, o_ref,
                    buf0_ref, buf1_ref, sems):
    i = pl.program_id(0)
    tm = h_ref.shape[0]
    slot = i % 2

    def row_copy(buf_ref, sl, t, d):
        return pltpu.make_async_copy(y_ref.at[pl.ds(d, 1)], buf_ref.at[sl, pl.ds(t, 1)], sems.at[sl])

    def gather(da_ref, db_ref, sl):
        def issue(t, carry):
            row_copy(buf0_ref, sl, t, da_ref[0, t]).start()
            row_copy(buf1_ref, sl, t, db_ref[0, t]).start()
            return carry

        lax.fori_loop(0, tm, issue, 0, unroll=8)

    @pl.when(i == 0)
    def _():
        gather(d0_ref, d1_ref, slot)

    @pl.when(i + 1 < pl.num_programs(0))
    def _():
        gather(d0n_ref, d1n_ref, 1 - slot)

    def drain(t, carry):
        row_copy(buf0_ref, slot, t, d0_ref[0, t]).wait()
        row_copy(buf1_ref, slot, t, d1_ref[0, t]).wait()
        return carry

    lax.fori_loop(0, tm, drain, 0, unroll=8)

    route = route_ref[...]
    moe = buf0_ref[slot] * route[:, 0:1] + buf1_ref[slot] * route[:, 1:2]
    h = h_ref[...] + gt2_ref[0] * moe
    o_ref[...] = _rms_scale(h) * gf_ref[...]


def _combine(d0, d1, route, h, gt2, gf, y, seq, tm):
    N, D = h.shape
    nt = N // tm
    per_seq = seq // tm
    cur = functools.partial(pl.BlockSpec, (None, 1, tm), lambda i: (i, 0, 0), memory_space=pltpu.SMEM)
    nxt = functools.partial(pl.BlockSpec, (None, 1, tm), lambda i: (jnp.minimum(i + 1, nt - 1), 0, 0),
                            memory_space=pltpu.SMEM)
    d0 = d0.reshape(nt, 1, tm)
    d1 = d1.reshape(nt, 1, tm)
    return pl.pallas_call(
        _combine_kernel,
        out_shape=jax.ShapeDtypeStruct((N, D), F32),
        grid=(nt,),
        in_specs=[cur(), cur(), nxt(), nxt(),
                  pl.BlockSpec((tm, LANES), lambda i: (i, 0)),
                  pl.BlockSpec((tm, D), lambda i: (i, 0)),
                  pl.BlockSpec((1, 1, D), lambda i: (i // per_seq, 0, 0)),
                  pl.BlockSpec((1, D), lambda i: (0, 0)),
                  pl.BlockSpec(memory_space=pl.ANY)],
        out_specs=pl.BlockSpec((tm, D), lambda i: (i, 0)),
        scratch_shapes=[pltpu.VMEM((2, tm, D), F32), pltpu.VMEM((2, tm, D), F32),
                        pltpu.SemaphoreType.DMA((2,))],
        compiler_params=pltpu.CompilerParams(dimension_semantics=("arbitrary",),
                                             vmem_limit_bytes=VMEM_LIMIT),
        name="combine",
    )(d0, d1, d0, d1, route, h, gt2, gf, y)


def _layer(h2, mod, batch, seq, norm1_g, w_in, conv_w, a_log, dt_bias, dn_norm_g, pool_w, pool_scale,
           w_lift_a, w_lift_b, w_out, norm2_g, w_rg, b_rg, w_re, b_re, w_gate, w_up, w_down, final_g):
    N, D = h2.shape
    sh1, sc1, gt1, sh2, sc2, gt2 = [m.reshape(batch, 1, D) for m in jnp.split(mod, 6, axis=-1)]

    o_z = 3 * DN_WIDTH
    o_a = o_z + DN_WIDTH
    o_pu = o_a + 2 * DN_HEADS
    o_g = o_pu + POOL_WIDTH
    w_qkvz = w_in[:, :o_a].astype(BF16)
    w_ab = jnp.pad(w_in[:, o_a:o_pu], ((0, 0), (0, LANES - 2 * DN_HEADS))).astype(BF16)
    w_pu = w_in[:, o_pu:o_g].astype(BF16)
    w_g = w_in[:, o_g:].astype(BF16)

    tm = min(256, seq)
    qkv, z, ab, pu, ga, gb = _inproj(h2, norm1_g.reshape(1, D), sc1, sh1, w_qkvz, w_ab, w_pu, w_g, seq, tm)

    ts = min(256, seq)
    ya = _deltanet(qkv, z, ab, conv_w, a_log, dt_bias, dn_norm_g, batch, seq, ts)

    n_r = MOE_GROUPS + MOE_EXPERTS
    w_router = jnp.pad(jnp.concatenate([w_rg, w_re], axis=1), ((0, 0), (0, LANES - n_r)))
    wr_hi = w_router.astype(BF16)
    wr_lo = (w_router - wr_hi.astype(F32)).astype(BF16)
    w_router = jnp.concatenate([wr_hi, wr_hi, wr_lo], axis=0)
    b_router = jnp.pad(jnp.concatenate([b_rg, b_re]), (0, LANES - n_r)).reshape(1, LANES)
    h_mid, xn2, route, counts = _mix(
        ya, pu, ga, gb, h2, gt1, sc2, sh2, norm2_g.reshape(1, D), pool_w.astype(BF16),
        pool_scale.reshape(1, POOL_WIDTH), w_lift_a.astype(BF16), w_lift_b.astype(BF16),
        w_out.astype(BF16), w_router, b_router, seq, tm)

    cnt = counts[0, :MOE_EXPERTS].astype(jnp.int32)
    padded = (cnt + MOE_ROWS - 1) // MOE_ROWS * MOE_ROWS
    pend = jnp.cumsum(padded)
    pstart = pend - padded
    n_slots = 2 * N + MOE_EXPERTS * MOE_ROWS
    nb = n_slots // MOE_ROWS
    nused = (pend[-1] // MOE_ROWS).astype(jnp.int32).reshape(1)
    blk_row0 = jnp.arange(nb, dtype=jnp.int32) * MOE_ROWS
    blk_e = jnp.minimum(jnp.sum((pend[None, :] <= blk_row0[:, None]).astype(jnp.int32), axis=1),
                        MOE_EXPERTS - 1)
    pstart_row = jnp.pad(pstart.astype(F32), (0, LANES - MOE_EXPERTS)).reshape(1, LANES)

    dest = _dest(route, pstart_row, min(2048, N))
    d0 = dest[:, 0]
    d1 = dest[:, 1]
    xs = _dispatch(pend.astype(jnp.int32), padded, nused, d0, d1, xn2, n_slots, min(1024, N))
    y = _experts(blk_e, nused, xs, w_gate.astype(BF16), w_up.astype(BF16), w_down.astype(BF16))
    return _combine(d0, d1, route, h_mid, gt2, final_g.reshape(1, D), y, seq, min(512, seq))


def kernel(x, c, w_ada, b_ada, norm1_g, w_in, conv_w, a_log, dt_bias, dn_norm_g, pool_w, pool_scale,
           w_lift_a, w_lift_b, w_out, norm2_g, w_router_group, b_router_group, w_router_expert,
           b_router_expert, w_gate, w_up, w_down, final_norm_g):
    batch, seq, D = x.shape
    depth = w_ada.shape[0]
    assert depth == 1, "the combine kernel fuses the final rmsnorm, so exactly one layer is supported"
    assert seq % DN_CHUNK == 0 and D % LANES == 0
    h2 = x.reshape(batch * seq, D)
    l = 0
    mod = _ada(c, w_ada[l], b_ada[l])
    out = _layer(h2, mod, batch, seq, norm1_g[l], w_in[l], conv_w[l], a_log[l], dt_bias[l], dn_norm_g[l],
                 pool_w[l], pool_scale[l], w_lift_a[l], w_lift_b[l], w_out[l], norm2_g[l],
                 w_router_group[l], b_router_group[l], w_router_expert[l], b_router_expert[l],
                 w_gate[l], w_up[l], w_down[l], final_norm_g)
    return out.reshape(batch, seq, D)
```

```python
import functools

import jax
import jax.numpy as jnp
from jax import lax
from jax.experimental import pallas as pl
from jax.experimental.pallas import tpu as pltpu

F32 = jnp.float32
BF16 = jnp.bfloat16
HIGHEST = lax.Precision.HIGHEST

EPS = 1e-6
LANES = 128
SUBLANES = 8
DN_HEADS = 4
DN_HEAD_DIM = 128
DN_WIDTH = DN_HEADS * DN_HEAD_DIM
DN_CONV = 4
DN_CHUNK = 64
DN_GROUP = 4
DN_WEAVE = 3
POOL_WINDOWS = (2, 4, 8, 16)
POOL_GROUP_DIM = 128
POOL_WIDTH = POOL_GROUP_DIM * len(POOL_WINDOWS)
POOL_HIST = 16
MOE_GROUPS = 4
MOE_EXPERTS_PER_GROUP = 8
MOE_EXPERTS = MOE_GROUPS * MOE_EXPERTS_PER_GROUP
MOE_ROWS = 512
INPROJ_SUB = 256
VMEM_LIMIT = 56 * 1024 * 1024


def _sigmoid(x):
    return 1.0 / (1.0 + jnp.exp(-x))


def _silu(x):
    return x * _sigmoid(x)


def _dot(a, b):
    return jnp.dot(a, b, preferred_element_type=F32)


def _dot_f32(a, b):
    return jnp.dot(a, b, preferred_element_type=F32, precision=HIGHEST)


def _dot_nt(a, b):
    return lax.dot_general(a, b, (((1,), (1,)), ((), ())), preferred_element_type=F32)


def _dot_tn(a, b):
    return lax.dot_general(a, b, (((0,), (0,)), ((), ())), preferred_element_type=F32)


def _split_dot(a, b):
    a_hi = a.astype(BF16)
    a_lo = (a - a_hi.astype(F32)).astype(BF16)
    b_hi = b.astype(BF16)
    b_lo = (b - b_hi.astype(F32)).astype(BF16)
    return _dot(a_hi, b_hi) + (_dot(a_hi, b_lo) + _dot(a_lo, b_hi))


def _rms_scale(x):
    return x * lax.rsqrt(jnp.mean(x * x, axis=-1, keepdims=True) + EPS)


def _ada_kernel(c_ref, w_ref, b_ref, o_ref):
    o_ref[...] = _dot_f32(_silu(c_ref[...]), w_ref[...]) + b_ref[...]


def _ada(c, w_ada, b_ada):
    B, D = c.shape
    n_out = w_ada.shape[1]
    tn = D
    return pl.pallas_call(
        _ada_kernel,
        out_shape=jax.ShapeDtypeStruct((B, n_out), F32),
        grid=(n_out // tn,),
        in_specs=[pl.BlockSpec((B, D), lambda j: (0, 0)),
                  pl.BlockSpec((D, tn), lambda j: (0, j)),
                  pl.BlockSpec((1, tn), lambda j: (0, j))],
        out_specs=pl.BlockSpec((B, tn), lambda j: (0, j)),
        compiler_params=pltpu.CompilerParams(dimension_semantics=("arbitrary",),
                                             vmem_limit_bytes=VMEM_LIMIT),
        name="ada",
    )(c, w_ada, b_ada.reshape(1, n_out))


def _inproj_kernel(x_ref, g_ref, sc_ref, sh_ref, wqkvz_ref, wab_ref, wpu_ref, wg_ref,
                   qkv_ref, z_ref, ab_ref, pu_ref, ga_ref, gb_ref):
    n_qkv = qkv_ref.shape[1]
    d = ga_ref.shape[1]
    tm = x_ref.shape[0]
    sub = min(INPROJ_SUB, tm)
    for r0 in range(0, tm, sub):
        rows = slice(r0, r0 + sub)
        x = x_ref[rows, :]
        xn = _rms_scale(x) * g_ref[...] * (1.0 + sc_ref[0]) + sh_ref[0]
        xb = xn.astype(BF16)
        qkvz = _dot(xb, wqkvz_ref[...])
        qkv_ref[rows, :] = qkvz[:, :n_qkv]
        z_ref[rows, :] = qkvz[:, n_qkv:]
        ab_ref[rows, :] = _dot(xb, wab_ref[...])
        pu_ref[rows, :] = _dot(xb, wpu_ref[...])
        gates = _dot(xb, wg_ref[...])
        ga_ref[rows, :] = gates[:, :d]
        gb_ref[rows, :] = gates[:, d:]


def _inproj(x2, g, sc, sh, w_qkvz, w_ab, w_pu, w_g, seq, tm):
    N, D = x2.shape
    per_seq = seq // tm
    n_qkv = 3 * DN_WIDTH

    def row(i):
        return (i, 0)

    def mod(i):
        return (i // per_seq, 0, 0)

    def const(i):
        return (0, 0)

    outs = [(N, n_qkv), (N, DN_WIDTH), (N, LANES), (N, POOL_WIDTH), (N, D), (N, D)]
    return pl.pallas_call(
        _inproj_kernel,
        out_shape=[jax.ShapeDtypeStruct(s, F32) for s in outs],
        grid=(N // tm,),
        in_specs=[pl.BlockSpec((tm, D), row),
                  pl.BlockSpec((1, D), const),
                  pl.BlockSpec((1, 1, D), mod),
                  pl.BlockSpec((1, 1, D), mod),
                  pl.BlockSpec(w_qkvz.shape, const),
                  pl.BlockSpec(w_ab.shape, const),
                  pl.BlockSpec(w_pu.shape, const),
                  pl.BlockSpec(w_g.shape, const)],
        out_specs=[pl.BlockSpec((tm, s[1]), row) for s in outs],
        compiler_params=pltpu.CompilerParams(dimension_semantics=("arbitrary",),
                                             vmem_limit_bytes=VMEM_LIMIT),
        name="inproj",
    )(x2, g, sc, sh, w_qkvz, w_ab, w_pu, w_g)


def _unit_lower_inverses(lowers, eye, ri, ci):
    def mm(a, b):
        return _dot(a.astype(BF16), b.astype(BF16))

    base = 3
    same = (ri >> base) == (ci >> base)
    ld = [jnp.where(same, l, 0.0) for l in lowers]
    a0 = [eye - x for x in ld]
    p1 = [mm(x, x) for x in ld]
    yield
    x1 = [a + mm(a, p) for a, p in zip(a0, p1)]
    yield
    p2 = [mm(p, p) for p in p1]
    yield
    t = [x + mm(x, p) for x, p in zip(x1, p2)]
    yield
    sh = base
    while (1 << sh) < DN_CHUNK:
        merge = ((ri >> (sh + 1)) == (ci >> (sh + 1))) & ((ri >> sh) != (ci >> sh))
        y = [mm(jnp.where(merge, l, 0.0), x) for l, x in zip(lowers, t)]
        yield
        t = [x - mm(x, yy) for x, yy in zip(t, y)]
        yield
        sh += 1
    resid = [(eye - x) - _split_dot(l, x) for l, x in zip(lowers, t)]
    yield
    return [x + mm(x, r) for x, r in zip(t, resid)]


def _stack_heads(x, r0, c0, width):
    return jnp.concatenate(
        [x[r0:r0 + DN_CHUNK, c0 + h * width:c0 + (h + 1) * width] for h in range(DN_HEADS)], axis=0)


def _deltanet_kernel(qkv_ref, prev_ref, z_ref, ab_ref, convw_ref, alog_ref, dtb_ref, dng_ref,
                     ya_ref, state_ref):
    s = pl.program_id(1)
    ts = qkv_ref.shape[0]
    C = DN_CHUNK
    hd = DN_HEAD_DIM
    H = DN_HEADS
    HC = H * C

    @pl.when(s == 0)
    def _():
        state_ref[...] = jnp.zeros_like(state_ref)

    tile = qkv_ref[...]
    prev = jnp.where(s == 0, 0.0, prev_ref[...])
    xf = jnp.concatenate([prev, tile], axis=0)
    cw = convw_ref[...]
    acc = tile * cw[DN_CONV - 1:DN_CONV]
    for sft in range(1, DN_CONV):
        shifted = pltpu.roll(xf, sft, 0)[SUBLANES:SUBLANES + ts]
        acc = acc + shifted * cw[DN_CONV - 1 - sft:DN_CONV - sft]
    act = _silu(acc)

    ab = ab_ref[...]
    neg_a = -jnp.exp(alog_ref[...])
    xs = ab + dtb_ref[...]
    softplus = jnp.maximum(xs, 0.0) + jnp.log1p(jnp.exp(-jnp.abs(xs)))
    g_all = neg_a * softplus
    beta_all = _sigmoid(ab)

    zg = _silu(z_ref[...])
    dng = dng_ref[...]

    ri = lax.broadcasted_iota(jnp.int32, (HC, HC), 0)
    ci = lax.broadcasted_iota(jnp.int32, (HC, HC), 1)
    log2_c = C.bit_length() - 1
    same_head = (ri >> log2_c) == (ci >> log2_c)
    causal = same_head & (ri >= ci)
    strict = same_head & (ri > ci)
    eye = jnp.where(ri == ci, 1.0, 0.0).astype(F32)
    r64 = lax.broadcasted_iota(jnp.int32, (C, C), 0)
    c64 = lax.broadcasted_iota(jnp.int32, (C, C), 1)
    tril = jnp.where(r64 >= c64, 1.0, 0.0).astype(F32)

    def chunk_prep(chunks, out):
        gcum = [_dot_f32(tril, g_all[c * C:(c + 1) * C]) for c in chunks]
        yield
        gc_col = [_stack_heads(g, 0, 0, 1) for g in gcum]
        gc_row = [jnp.concatenate([g.T[h:h + 1, :] for h in range(H)], axis=1) for g in gcum]
        g_last = [jnp.concatenate([jnp.broadcast_to(g[C - 1:C, h:h + 1], (C, 1)) for h in range(H)], axis=0)
                  for g in gcum]
        yield
        beta = [_stack_heads(beta_all, c * C, H, 1) for c in chunks]
        q = [_stack_heads(act, c * C, 0, hd) for c in chunks]
        k = [_stack_heads(act, c * C, DN_WIDTH, hd) for c in chunks]
        v = [_stack_heads(act, c * C, 2 * DN_WIDTH, hd) for c in chunks]
        q = [x * lax.rsqrt(jnp.sum(x * x, axis=-1, keepdims=True) + EPS) * (hd ** -0.5) for x in q]
        yield
        k = [x * lax.rsqrt(jnp.sum(x * x, axis=-1, keepdims=True) + EPS) for x in k]
        yield
        decay = [jnp.where(causal, jnp.exp(jnp.where(causal, gc - gr, 0.0)), 0.0)
                 for gc, gr in zip(gc_col, gc_row)]
        yield
        eg = [jnp.exp(gc) for gc in gc_col]
        kb = [x * b for x, b in zip(k, beta)]
        k16 = [x.astype(BF16) for x in k]
        lower = [jnp.where(strict, _dot_nt(a.astype(BF16), b) * d, 0.0) for a, b, d in zip(kb, k16, decay)]
        yield
        t = yield from _unit_lower_inverses(lower, eye, ri, ci)
        t16 = [x.astype(BF16) for x in t]
        yield
        uw = [_dot(t, jnp.concatenate([x * b, y * e], axis=1).astype(BF16))
              for t, x, b, y, e in zip(t16, v, beta, kb, eg)]
        yield
        attn = [(_dot_nt(x.astype(BF16), y) * d).astype(BF16) for x, y, d in zip(q, k16, decay)]
        yield
        for n, c in enumerate(chunks):
            out[c] = dict(u=uw[n][:, :hd], w=uw[n][:, hd:].astype(BF16), attn=attn[n],
                          qd=(q[n] * eg[n]).astype(BF16),
                          kd=(k[n] * jnp.exp(g_last[n] - gc_col[n])).astype(BF16),
                          el=[jnp.exp(gcum[n][C - 1:C, h:h + 1]) for h in range(H)])

    def recurrence(chunks, prepared):
        for c in chunks:
            pc = prepared[c]
            r0 = c * C
            v_new = []
            q_state = []
            for h in range(H):
                rows = slice(h * C, (h + 1) * C)
                st16 = state_ref[h].astype(BF16)
                both = _dot(jnp.concatenate([pc["w"][rows], pc["qd"][rows]], axis=0), st16)
                v_new.append(pc["u"][rows] - both[:C])
                q_state.append(both[C:])
            yield
            vn16 = jnp.concatenate(v_new, axis=0).astype(BF16)
            intra = _dot(pc["attn"], vn16)
            for h in range(H):
                rows = slice(h * C, (h + 1) * C)
                state_ref[h] = state_ref[h] * pc["el"][h] + _dot_tn(pc["kd"][rows], vn16[rows])
            yield
            for h in range(H):
                rows = slice(h * C, (h + 1) * C)
                o = q_state[h] + intra[rows]
                y = _rms_scale(o) * dng * zg[r0:r0 + C, h * hd:(h + 1) * hd]
                ya_ref[r0:r0 + C, h * hd:(h + 1) * hd] = y.astype(ya_ref.dtype)
            yield

    n_chunks = ts // C
    groups = [list(range(g0, min(g0 + DN_GROUP, n_chunks))) for g0 in range(0, n_chunks, DN_GROUP)]
    prepared = {}
    pending = iter(())
    for grp in groups:
        n = 0
        for _ in chunk_prep(grp, prepared):
            n += 1
            if n % DN_WEAVE == 0:
                next(pending, None)
        for _ in pending:
            pass
        pending = recurrence(grp, prepared)
    for _ in pending:
        pass


def _deltanet(qkv, z, ab, conv_w, a_log, dt_bias, dn_norm_g, batch, seq, ts):
    N = qkv.shape[0]
    per_seq = seq // ts
    n_qkv = qkv.shape[1]
    pad = LANES - DN_HEADS
    alog_row = jnp.pad(a_log, (0, pad)).reshape(1, LANES)
    dtb_row = jnp.pad(dt_bias, (0, pad)).reshape(1, LANES)

    def row(b, s):
        return (b * per_seq + s, 0)

    def prev(b, s):
        return (jnp.maximum((b * per_seq + s) * (ts // SUBLANES) - 1, 0), 0)

    def const(b, s):
        return (0, 0)

    return pl.pallas_call(
        _deltanet_kernel,
        out_shape=jax.ShapeDtypeStruct((N, DN_WIDTH), BF16),
        grid=(batch, per_seq),
        in_specs=[pl.BlockSpec((ts, n_qkv), row),
                  pl.BlockSpec((SUBLANES, n_qkv), prev),
                  pl.BlockSpec((ts, DN_WIDTH), row),
                  pl.BlockSpec((ts, LANES), row),
                  pl.BlockSpec(conv_w.shape, const),
                  pl.BlockSpec((1, LANES), const),
                  pl.BlockSpec((1, LANES), const),
                  pl.BlockSpec((1, DN_HEAD_DIM), const)],
        out_specs=pl.BlockSpec((ts, DN_WIDTH), row),
        scratch_shapes=[pltpu.VMEM((DN_HEADS, DN_HEAD_DIM, DN_HEAD_DIM), F32)],
        compiler_params=pltpu.CompilerParams(dimension_semantics=("arbitrary", "arbitrary"),
                                             vmem_limit_bytes=VMEM_LIMIT),
        name="deltanet",
    )(qkv, qkv, z, ab, conv_w, alog_row, dtb_row, dn_norm_g.reshape(1, DN_HEAD_DIM))


def _mix_kernel(ya_ref, pu_ref, pprev_ref, ga_ref, gb_ref, x_ref, gt1_ref, sc2_ref, sh2_ref, g2_ref,
                poolw_ref, pscale_ref, wla_ref, wlb_ref, wout_ref, wr_ref, br_ref,
                h_ref, xn_ref, route_ref, counts_ref, carry_ref, *, per_seq):
    i = pl.program_id(0)
    tm = x_ref.shape[0]
    first = (i % per_seq) == 0

    @pl.when(i == 0)
    def _():
        carry_ref[...] = jnp.zeros_like(carry_ref)

    pu = pu_ref[...]
    hist = jnp.where(first, 0.0, pprev_ref[...])
    pf = jnp.concatenate([hist, pu], axis=0)
    pos = (i % per_seq) * tm + lax.broadcasted_iota(jnp.int32, (tm, 1), 0)
    pscale = pscale_ref[...]
    yb_parts = []
    for gi, win in enumerate(POOL_WINDOWS):
        lo = gi * POOL_GROUP_DIM
        xg = pf[:, lo:lo + POOL_GROUP_DIM]
        acc = xg
        span = 1
        while span < win:
            acc = acc + pltpu.roll(acc, span, 0)
            span *= 2
        cnt = jnp.minimum(pos + 1, win).astype(F32)
        pooled = acc[POOL_HIST:] / cnt - pu[:, lo:lo + POOL_GROUP_DIM]
        yg = _dot(pooled.astype(BF16), poolw_ref[gi])
        yb_parts.append(yg * pscale[:, lo:lo + POOL_GROUP_DIM])
    yb = jnp.concatenate(yb_parts, axis=1)

    lift_a = _dot(ya_ref[...], wla_ref[...])
    lift_b = _dot(yb.astype(BF16), wlb_ref[...])
    mixed = _sigmoid(ga_ref[...]) * lift_a + _sigmoid(gb_ref[...]) * lift_b
    h = x_ref[...] + gt1_ref[0] * _dot(mixed.astype(BF16), wout_ref[...])
    h_ref[...] = h
    xn = _rms_scale(h) * g2_ref[...] * (1.0 + sc2_ref[0]) + sh2_ref[0]
    xn_ref[...] = xn

    xn_hi = xn.astype(BF16)
    xn_lo = (xn - xn_hi.astype(F32)).astype(BF16)
    logits = _dot(jnp.concatenate([xn_hi, xn_lo, xn_hi], axis=1), wr_ref[...]) + br_ref[...]
    li = lax.broadcasted_iota(jnp.int32, (tm, LANES), 1).astype(F32)
    neg = -jnp.inf
    big = float(LANES)
    is_g = li < MOE_GROUPS
    lg = jnp.where(is_g, logits, neg)
    mg = jnp.max(lg, axis=-1, keepdims=True)
    p_grp = 1.0 / jnp.sum(jnp.where(is_g, jnp.exp(lg - mg), 0.0), axis=-1, keepdims=True)
    grp = jnp.min(jnp.where(lg == mg, li, big), axis=-1, keepdims=True)
    e_lo = MOE_GROUPS + MOE_EXPERTS_PER_GROUP * grp
    is_e = (li >= e_lo) & (li < e_lo + MOE_EXPERTS_PER_GROUP)
    le = jnp.where(is_e, logits, neg)
    m1 = jnp.max(le, axis=-1, keepdims=True)
    i1 = jnp.min(jnp.where(le == m1, li, big), axis=-1, keepdims=True)
    le2 = jnp.where(li == i1, neg, le)
    m2 = jnp.max(le2, axis=-1, keepdims=True)
    i2 = jnp.min(jnp.where(le2 == m2, li, big), axis=-1, keepdims=True)
    e2 = jnp.exp(m2 - m1)
    w1 = p_grp / (1.0 + e2)
    w2 = p_grp * e2 / (1.0 + e2)
    eid1 = i1 - MOE_GROUPS
    eid2 = i2 - MOE_GROUPS

    oh1 = li == eid1
    oh2 = li == eid2
    onehot = jnp.where(oh1 | oh2, 1.0, 0.0)
    ri = lax.broadcasted_iota(jnp.int32, (tm, tm), 0)
    ci = lax.broadcasted_iota(jnp.int32, (tm, tm), 1)
    below = jnp.where(ri > ci, 1.0, 0.0).astype(BF16)
    carry = carry_ref[0:1, :]
    before = _dot(below, onehot.astype(BF16)) + carry
    rank1 = jnp.sum(jnp.where(oh1, before, 0.0), axis=-1, keepdims=True)
    rank2 = jnp.sum(jnp.where(oh2, before, 0.0), axis=-1, keepdims=True)
    new_carry = carry + jnp.sum(onehot, axis=0, keepdims=True)
    carry_ref[...] = jnp.broadcast_to(new_carry, carry_ref.shape)
    counts_ref[...] = jnp.broadcast_to(new_carry, counts_ref.shape)

    route = jnp.where(li == 0.0, w1, 0.0)
    for lane, val in ((1, w2), (2, eid1), (3, eid2), (4, rank1), (5, rank2)):
        route = jnp.where(li == float(lane), val, route)
    route_ref[...] = route


def _mix(ya, pu, ga, gb, x2, gt1, sc2, sh2, g2, pool_w, pool_scale, w_lift_a, w_lift_b, w_out,
         w_router, b_router, seq, tm):
    N, D = x2.shape
    per_seq = seq // tm

    def row(i):
        return (i, 0)

    def prev(i):
        return (jnp.maximum(i * (tm // POOL_HIST) - 1, 0), 0)

    def mod(i):
        return (i // per_seq, 0, 0)

    def const(i):
        return (0, 0)

    return pl.pallas_call(
        functools.partial(_mix_kernel, per_seq=per_seq),
        out_shape=[jax.ShapeDtypeStruct((N, D), F32),
                   jax.ShapeDtypeStruct((N, D), F32),
                   jax.ShapeDtypeStruct((N, LANES), F32),
                   jax.ShapeDtypeStruct((SUBLANES, LANES), F32)],
        grid=(N // tm,),
        in_specs=[pl.BlockSpec((tm, DN_WIDTH), row),
                  pl.BlockSpec((tm, POOL_WIDTH), row),
                  pl.BlockSpec((POOL_HIST, POOL_WIDTH), prev),
                  pl.BlockSpec((tm, D), row),
                  pl.BlockSpec((tm, D), row),
                  pl.BlockSpec((tm, D), row),
                  pl.BlockSpec((1, 1, D), mod),
                  pl.BlockSpec((1, 1, D), mod),
                  pl.BlockSpec((1, 1, D), mod),
                  pl.BlockSpec((1, D), const),
                  pl.BlockSpec(pool_w.shape, lambda i: (0, 0, 0)),
                  pl.BlockSpec((1, POOL_WIDTH), const),
                  pl.BlockSpec(w_lift_a.shape, const),
                  pl.BlockSpec(w_lift_b.shape, const),
                  pl.BlockSpec(w_out.shape, const),
                  pl.BlockSpec(w_router.shape, const),
                  pl.BlockSpec((1, LANES), const)],
        out_specs=[pl.BlockSpec((tm, D), row),
                   pl.BlockSpec((tm, D), row),
                   pl.BlockSpec((tm, LANES), row),
                   pl.BlockSpec((SUBLANES, LANES), const)],
        scratch_shapes=[pltpu.VMEM((SUBLANES, LANES), F32)],
        compiler_params=pltpu.CompilerParams(dimension_semantics=("arbitrary",),
                                             vmem_limit_bytes=VMEM_LIMIT),
        name="mix",
    )(ya, pu, pu, ga, gb, x2, gt1, sc2, sh2, g2, pool_w, pool_scale, w_lift_a, w_lift_b, w_out,
      w_router, b_router)


def _dest_kernel(route_ref, pstart_ref, dest_ref):
    route = route_ref[...]
    tm = route.shape[0]
    li = lax.broadcasted_iota(jnp.int32, (tm, LANES), 1).astype(F32)
    pstart = pstart_ref[...]
    out = jnp.zeros((tm, LANES), F32)
    for k in range(2):
        eid = route[:, 2 + k:3 + k]
        rank = route[:, 4 + k:5 + k]
        slot = jnp.sum(jnp.where(li == eid, pstart, 0.0), axis=-1, keepdims=True) + rank
        out = jnp.where(li == float(k), slot, out)
    dest_ref[...] = out.astype(jnp.int32)


def _dest(route, pstart_row, tm):
    N = route.shape[0]
    return pl.pallas_call(
        _dest_kernel,
        out_shape=jax.ShapeDtypeStruct((N, LANES), jnp.int32),
        grid=(N // tm,),
        in_specs=[pl.BlockSpec((tm, LANES), lambda i: (i, 0)),
                  pl.BlockSpec((1, LANES), lambda i: (0, 0))],
        out_specs=pl.BlockSpec((tm, LANES), lambda i: (i, 0)),
        compiler_params=pltpu.CompilerParams(dimension_semantics=("arbitrary",)),
        name="dest",
    )(route, pstart_row)


def _dispatch_kernel(pend_ref, padded_ref, nused_ref, d0_ref, d1_ref, x_ref, xs_ref, zero_ref, sem, zsem):
    tm = x_ref.shape[0]
    nb = xs_ref.shape[0] // MOE_ROWS

    @pl.when(pl.program_id(0) == 0)
    def _():
        zero_ref[...] = jnp.zeros_like(zero_ref)

        def zero_block(row0):
            return pltpu.make_async_copy(zero_ref, xs_ref.at[pl.ds(row0, MOE_ROWS)], zsem)

        def tail_block(e):
            return zero_block(pl.multiple_of(pend_ref[e] - MOE_ROWS, MOE_ROWS))

        def unused_block(j):
            return zero_block(pl.multiple_of(j * MOE_ROWS, MOE_ROWS))

        def for_each_zero_block(act):
            def per_expert(e, carry):
                @pl.when(padded_ref[e] > 0)
                def _():
                    act(tail_block(e))
                return carry

            def per_unused(j, carry):
                act(unused_block(j))
                return carry

            lax.fori_loop(0, MOE_EXPERTS, per_expert, 0)
            lax.fori_loop(nused_ref[0], nb, per_unused, 0)

        for_each_zero_block(lambda cp: cp.start())
        for_each_zero_block(lambda cp: cp.wait())

    def row_copy(t, d):
        return pltpu.make_async_copy(x_ref.at[pl.ds(t, 1)], xs_ref.at[pl.ds(d, 1)], sem)

    def issue(t, carry):
        row_copy(t, d0_ref[0, t]).start()
        row_copy(t, d1_ref[0, t]).start()
        return carry

    lax.fori_loop(0, tm, issue, 0, unroll=8)

    def drain(t, carry):
        row_copy(t, d0_ref[0, t]).wait()
        row_copy(t, d1_ref[0, t]).wait()
        return carry

    lax.fori_loop(0, tm, drain, 0, unroll=8)


def _dispatch(pend, padded, nused, d0, d1, xn, n_slots, tm):
    N, D = xn.shape
    nt = N // tm
    smem = functools.partial(pl.BlockSpec, (None, 1, tm), lambda i, *_: (i, 0, 0),
                             memory_space=pltpu.SMEM)
    return pl.pallas_call(
        _dispatch_kernel,
        out_shape=jax.ShapeDtypeStruct((n_slots, D), xn.dtype),
        grid_spec=pltpu.PrefetchScalarGridSpec(
            num_scalar_prefetch=3,
            grid=(nt,),
            in_specs=[smem(), smem(), pl.BlockSpec((tm, D), lambda i, *_: (i, 0))],
            out_specs=pl.BlockSpec(memory_space=pl.ANY),
            scratch_shapes=[pltpu.VMEM((MOE_ROWS, D), xn.dtype), pltpu.SemaphoreType.DMA,
                            pltpu.SemaphoreType.DMA]),
        compiler_params=pltpu.CompilerParams(dimension_semantics=("arbitrary",),
                                             vmem_limit_bytes=VMEM_LIMIT),
        name="dispatch",
    )(pend, padded, nused, d0.reshape(nt, 1, tm), d1.reshape(nt, 1, tm), xn)


def _experts_kernel(blk_e_ref, nused_ref, x_ref, wg_ref, wu_ref, wd_ref, y_ref):
    del blk_e_ref
    used = pl.program_id(0) < nused_ref[0]

    @pl.when(used)
    def _():
        x = x_ref[...].astype(BF16)
        hid = _silu(_dot(x, wg_ref[...])) * _dot(x, wu_ref[...])
        y_ref[...] = _dot(hid.astype(BF16), wd_ref[...])

    @pl.when(jnp.logical_not(used))
    def _():
        y_ref[...] = jnp.zeros_like(y_ref)


def _experts(blk_e, nused, xs, w_gate, w_up, w_down):
    P, D = xs.shape
    nb = P // MOE_ROWS
    dff = w_gate.shape[2]

    def rows(j, blk_e_ref, nused_ref):
        return (jnp.maximum(jnp.minimum(j, nused_ref[0] - 1), 0), 0)

    def wsel(j, blk_e_ref, nused_ref):
        return (blk_e_ref[j], 0, 0)

    return pl.pallas_call(
        _experts_kernel,
        out_shape=jax.ShapeDtypeStruct((P, D), F32),
        grid_spec=pltpu.PrefetchScalarGridSpec(
            num_scalar_prefetch=2,
            grid=(nb,),
            in_specs=[pl.BlockSpec((MOE_ROWS, D), rows),
                      pl.BlockSpec((None, D, dff), wsel),
                      pl.BlockSpec((None, D, dff), wsel),
                      pl.BlockSpec((None, dff, D), wsel)],
            out_specs=pl.BlockSpec((MOE_ROWS, D), lambda j, blk_e_ref, nused_ref: (j, 0))),
        compiler_params=pltpu.CompilerParams(dimension_semantics=("arbitrary",),
                                             vmem_limit_bytes=VMEM_LIMIT),
        name="experts",
    )(blk_e, nused, xs, w_gate, w_up, w_down)


def _combine_kernel(d0_ref, d1_ref, d0n_ref, d1n_ref, route_ref, h_ref, gt2_ref, gf_ref, y_ref, o_ref,
                    buf0_ref, buf1_ref, sems):
    i = pl.program_id(0)
    tm = h_ref.shape[0]
    slot = i % 2

    def row_copy(buf_ref, sl, t, d):
        return pltpu.make_async_copy(y_ref.at[pl.ds(d, 1)], buf_ref.at[sl, pl.ds(t, 1)], sems.at[sl])

    def gather(da_ref, db_ref, sl):
        def issue(t, carry):
            row_copy(buf0_ref, sl, t, da_ref[0, t]).start()
            row_copy(buf1_ref, sl, t, db_ref[0, t]).start()
            return carry

        lax.fori_loop(0, tm, issue, 0, unroll=8)

    @pl.when(i == 0)
    def _():
        gather(d0_ref, d1_ref, slot)

    @pl.when(i + 1 < pl.num_programs(0))
    def _():
        gather(d0n_ref, d1n_ref, 1 - slot)

    def drain(t, carry):
        row_copy(buf0_ref, slot, t, d0_ref[0, t]).wait()
        row_copy(buf1_ref, slot, t, d1_ref[0, t]).wait()
        return carry

    lax.fori_loop(0, tm, drain, 0, unroll=8)

    route = route_ref[...]
    moe = buf0_ref[slot] * route[:, 0:1] + buf1_ref[slot] * route[:, 1:2]
    h = h_ref[...] + gt2_ref[0] * moe
    o_ref[...] = _rms_scale(h) * gf_ref[...]


def _combine(d0, d1, route, h, gt2, gf, y, seq, tm):
    N, D = h.shape
    nt = N // tm
    per_seq = seq // tm
    cur = functools.partial(pl.BlockSpec, (None, 1, tm), lambda i: (i, 0, 0), memory_space=pltpu.SMEM)
    nxt = functools.partial(pl.BlockSpec, (None, 1, tm), lambda i: (jnp.minimum(i + 1, nt - 1), 0, 0),
                            memory_space=pltpu.SMEM)
    d0 = d0.reshape(nt, 1, tm)
    d1 = d1.reshape(nt, 1, tm)
    return pl.pallas_call(
        _combine_kernel,
        out_shape=jax.ShapeDtypeStruct((N, D), F32),
        grid=(nt,),
        in_specs=[cur(), cur(), nxt(), nxt(),
                  pl.BlockSpec((tm, LANES), lambda i: (i, 0)),
                  pl.BlockSpec((tm, D), lambda i: (i, 0)),
                  pl.BlockSpec((1, 1, D), lambda i: (i // per_seq, 0, 0)),
                  pl.BlockSpec((1, D), lambda i: (0, 0)),
                  pl.BlockSpec(memory_space=pl.ANY)],
        out_specs=pl.BlockSpec((tm, D), lambda i: (i, 0)),
        scratch_shapes=[pltpu.VMEM((2, tm, D), F32), pltpu.VMEM((2, tm, D), F32),
                        pltpu.SemaphoreType.DMA((2,))],
        compiler_params=pltpu.CompilerParams(dimension_semantics=("arbitrary",),
                                             vmem_limit_bytes=VMEM_LIMIT),
        name="combine",
    )(d0, d1, d0, d1, route, h, gt2, gf, y)


def _layer(h2, mod, batch, seq, norm1_g, w_in, conv_w, a_log, dt_bias, dn_norm_g, pool_w, pool_scale,
           w_lift_a, w_lift_b, w_out, norm2_g, w_rg, b_rg, w_re, b_re, w_gate, w_up, w_down, final_g):
    N, D = h2.shape
    sh1, sc1, gt1, sh2, sc2, gt2 = [m.reshape(batch, 1, D) for m in jnp.split(mod, 6, axis=-1)]

    o_z = 3 * DN_WIDTH
    o_a = o_z + DN_WIDTH
    o_pu = o_a + 2 * DN_HEADS
    o_g = o_pu + POOL_WIDTH
    w_qkvz = w_in[:, :o_a].astype(BF16)
    w_ab = jnp.pad(w_in[:, o_a:o_pu], ((0, 0), (0, LANES - 2 * DN_HEADS))).astype(BF16)
    w_pu = w_in[:, o_pu:o_g].astype(BF16)
    w_g = w_in[:, o_g:].astype(BF16)

    tm = min(256, seq)
    qkv, z, ab, pu, ga, gb = _inproj(h2, norm1_g.reshape(1, D), sc1, sh1, w_qkvz, w_ab, w_pu, w_g, seq,
                                     min(512, seq))

    ts = min(512, seq)
    ya = _deltanet(qkv, z, ab, conv_w, a_log, dt_bias, dn_norm_g, batch, seq, ts)

    n_r = MOE_GROUPS + MOE_EXPERTS
    w_router = jnp.pad(jnp.concatenate([w_rg, w_re], axis=1), ((0, 0), (0, LANES - n_r)))
    wr_hi = w_router.astype(BF16)
    wr_lo = (w_router - wr_hi.astype(F32)).astype(BF16)
    w_router = jnp.concatenate([wr_hi, wr_hi, wr_lo], axis=0)
    b_router = jnp.pad(jnp.concatenate([b_rg, b_re]), (0, LANES - n_r)).reshape(1, LANES)
    h_mid, xn2, route, counts = _mix(
        ya, pu, ga, gb, h2, gt1, sc2, sh2, norm2_g.reshape(1, D), pool_w.astype(BF16),
        pool_scale.reshape(1, POOL_WIDTH), w_lift_a.astype(BF16), w_lift_b.astype(BF16),
        w_out.astype(BF16), w_router, b_router, seq, tm)

    cnt = counts[0, :MOE_EXPERTS].astype(jnp.int32)
    padded = (cnt + MOE_ROWS - 1) // MOE_ROWS * MOE_ROWS
    pend = jnp.cumsum(padded)
    pstart = pend - padded
    n_slots = 2 * N + MOE_EXPERTS * MOE_ROWS
    nb = n_slots // MOE_ROWS
    nused = (pend[-1] // MOE_ROWS).astype(jnp.int32).reshape(1)
    blk_row0 = jnp.arange(nb, dtype=jnp.int32) * MOE_ROWS
    blk_e = jnp.minimum(jnp.sum((pend[None, :] <= blk_row0[:, None]).astype(jnp.int32), axis=1),
                        MOE_EXPERTS - 1)
    pstart_row = jnp.pad(pstart.astype(F32), (0, LANES - MOE_EXPERTS)).reshape(1, LANES)

    dest = _dest(route, pstart_row, min(2048, N))
    d0 = dest[:, 0]
    d1 = dest[:, 1]
    xs = _dispatch(pend.astype(jnp.int32), padded, nused, d0, d1, xn2, n_slots, min(1024, N))
    y = _experts(blk_e, nused, xs, w_gate.astype(BF16), w_up.astype(BF16), w_down.astype(BF16))
    return _combine(d0, d1, route, h_mid, gt2, final_g.reshape(1, D), y, seq, min(512, seq))


def kernel(x, c, w_ada, b_ada, norm1_g, w_in, conv_w, a_log, dt_bias, dn_norm_g, pool_w, pool_scale,
           w_lift_a, w_lift_b, w_out, norm2_g, w_router_group, b_router_group, w_router_expert,
           b_router_expert, w_gate, w_up, w_down, final_norm_g):
    batch, seq, D = x.shape
    depth = w_ada.shape[0]
    assert depth == 1, "the combine kernel fuses the final rmsnorm, so exactly one layer is supported"
    assert seq % DN_CHUNK == 0 and D % LANES == 0
    h2 = x.reshape(batch * seq, D)
    l = 0
    mod = _ada(c, w_ada[l], b_ada[l])
    out = _layer(h2, mod, batch, seq, norm1_g[l], w_in[l], conv_w[l], a_log[l], dt_bias[l], dn_norm_g[l],
                 pool_w[l], pool_scale[l], w_lift_a[l], w_lift_b[l], w_out[l], norm2_g[l],
                 w_router_group[l], b_router_group[l], w_router_expert[l], b_router_expert[l],
                 w_gate[l], w_up[l], w_down[l], final_norm_g)
    return out.reshape(batch, seq, D)
```

```python
import functools

import jax
import jax.numpy as jnp
from jax import lax
from jax.experimental import pallas as pl
from jax.experimental.pallas import tpu as pltpu

F32 = jnp.float32
BF16 = jnp.bfloat16
HIGHEST = lax.Precision.HIGHEST

EPS = 1e-6
LANES = 128
SUBLANES = 8
DN_HEADS = 4
DN_HEAD_DIM = 128
DN_WIDTH = DN_HEADS * DN_HEAD_DIM
DN_CONV = 4
DN_CHUNK = 64
DN_GROUP = 4
DN_WEAVE = 2
POOL_WINDOWS = (2, 4, 8, 16)
POOL_GROUP_DIM = 128
POOL_WIDTH = POOL_GROUP_DIM * len(POOL_WINDOWS)
POOL_HIST = 16
MOE_GROUPS = 4
MOE_EXPERTS_PER_GROUP = 8
MOE_EXPERTS = MOE_GROUPS * MOE_EXPERTS_PER_GROUP
MOE_ROWS = 512
INPROJ_SUB = 256
VMEM_LIMIT = 56 * 1024 * 1024


def _sigmoid(x):
    return 1.0 / (1.0 + jnp.exp(-x))


def _silu(x):
    return x * _sigmoid(x)


def _dot(a, b):
    return jnp.dot(a, b, preferred_element_type=F32)


def _dot_f32(a, b):
    return jnp.dot(a, b, preferred_element_type=F32, precision=HIGHEST)


def _dot_nt(a, b):
    return lax.dot_general(a, b, (((1,), (1,)), ((), ())), preferred_element_type=F32)


def _dot_tn(a, b):
    return lax.dot_general(a, b, (((0,), (0,)), ((), ())), preferred_element_type=F32)


def _split_dot(a, b):
    a_hi = a.astype(BF16)
    a_lo = (a - a_hi.astype(F32)).astype(BF16)
    b_hi = b.astype(BF16)
    b_lo = (b - b_hi.astype(F32)).astype(BF16)
    return _dot(a_hi, b_hi) + (_dot(a_hi, b_lo) + _dot(a_lo, b_hi))


def _rms_scale(x):
    return x * lax.rsqrt(jnp.mean(x * x, axis=-1, keepdims=True) + EPS)


def _ada_kernel(c_ref, w_ref, b_ref, o_ref):
    o_ref[...] = _dot_f32(_silu(c_ref[...]), w_ref[...]) + b_ref[...]


def _ada(c, w_ada, b_ada):
    B, D = c.shape
    n_out = w_ada.shape[1]
    tn = D
    return pl.pallas_call(
        _ada_kernel,
        out_shape=jax.ShapeDtypeStruct((B, n_out), F32),
        grid=(n_out // tn,),
        in_specs=[pl.BlockSpec((B, D), lambda j: (0, 0)),
                  pl.BlockSpec((D, tn), lambda j: (0, j)),
                  pl.BlockSpec((1, tn), lambda j: (0, j))],
        out_specs=pl.BlockSpec((B, tn), lambda j: (0, j)),
        compiler_params=pltpu.CompilerParams(dimension_semantics=("arbitrary",),
                                             vmem_limit_bytes=VMEM_LIMIT),
        name="ada",
    )(c, w_ada, b_ada.reshape(1, n_out))


def _inproj_kernel(x_ref, g_ref, sc_ref, sh_ref, wqkvz_ref, wab_ref, wpu_ref, wg_ref,
                   qkv_ref, z_ref, ab_ref, pu_ref, ga_ref, gb_ref):
    n_qkv = qkv_ref.shape[1]
    d = ga_ref.shape[1]
    tm = x_ref.shape[0]
    sub = min(INPROJ_SUB, tm)
    for r0 in range(0, tm, sub):
        rows = slice(r0, r0 + sub)
        x = x_ref[rows, :]
        xn = _rms_scale(x) * g_ref[...] * (1.0 + sc_ref[0]) + sh_ref[0]
        xb = xn.astype(BF16)
        qkvz = _dot(xb, wqkvz_ref[...])
        qkv_ref[rows, :] = qkvz[:, :n_qkv]
        z_ref[rows, :] = qkvz[:, n_qkv:]
        ab_ref[rows, :] = _dot(xb, wab_ref[...])
        pu_ref[rows, :] = _dot(xb, wpu_ref[...])
        gates = _dot(xb, wg_ref[...])
        ga_ref[rows, :] = gates[:, :d]
        gb_ref[rows, :] = gates[:, d:]


def _inproj(x2, g, sc, sh, w_qkvz, w_ab, w_pu, w_g, seq, tm):
    N, D = x2.shape
    per_seq = seq // tm
    n_qkv = 3 * DN_WIDTH

    def row(i):
        return (i, 0)

    def mod(i):
        return (i // per_seq, 0, 0)

    def const(i):
        return (0, 0)

    outs = [(N, n_qkv), (N, DN_WIDTH), (N, LANES), (N, POOL_WIDTH), (N, D), (N, D)]
    return pl.pallas_call(
        _inproj_kernel,
        out_shape=[jax.ShapeDtypeStruct(s, F32) for s in outs],
        grid=(N // tm,),
        in_specs=[pl.BlockSpec((tm, D), row),
                  pl.BlockSpec((1, D), const),
                  pl.BlockSpec((1, 1, D), mod),
                  pl.BlockSpec((1, 1, D), mod),
                  pl.BlockSpec(w_qkvz.shape, const),
                  pl.BlockSpec(w_ab.shape, const),
                  pl.BlockSpec(w_pu.shape, const),
                  pl.BlockSpec(w_g.shape, const)],
        out_specs=[pl.BlockSpec((tm, s[1]), row) for s in outs],
        compiler_params=pltpu.CompilerParams(dimension_semantics=("arbitrary",),
                                             vmem_limit_bytes=VMEM_LIMIT),
        name="inproj",
    )(x2, g, sc, sh, w_qkvz, w_ab, w_pu, w_g)


def _unit_lower_inverses(lowers, eye, ri, ci):
    def mm(a, b):
        return _dot(a.astype(BF16), b.astype(BF16))

    base = 3
    same = (ri >> base) == (ci >> base)
    ld = [jnp.where(same, l, 0.0) for l in lowers]
    a0 = [eye - x for x in ld]
    p1 = [mm(x, x) for x in ld]
    yield
    x1 = [a + mm(a, p) for a, p in zip(a0, p1)]
    yield
    p2 = [mm(p, p) for p in p1]
    yield
    t = [x + mm(x, p) for x, p in zip(x1, p2)]
    yield
    sh = base
    while (1 << sh) < DN_CHUNK:
        merge = ((ri >> (sh + 1)) == (ci >> (sh + 1))) & ((ri >> sh) != (ci >> sh))
        y = [mm(jnp.where(merge, l, 0.0), x) for l, x in zip(lowers, t)]
        yield
        t = [x - mm(x, yy) for x, yy in zip(t, y)]
        yield
        sh += 1
    resid = [(eye - x) - _split_dot(l, x) for l, x in zip(lowers, t)]
    yield
    return [x + mm(x, r) for x, r in zip(t, resid)]


def _stack_heads(x, r0, c0, width):
    return jnp.concatenate(
        [x[r0:r0 + DN_CHUNK, c0 + h * width:c0 + (h + 1) * width] for h in range(DN_HEADS)], axis=0)


def _deltanet_kernel(qkv_ref, prev_ref, z_ref, ab_ref, convw_ref, alog_ref, dtb_ref, dng_ref,
                     ya_ref, state_ref):
    s = pl.program_id(1)
    ts = qkv_ref.shape[0]
    C = DN_CHUNK
    hd = DN_HEAD_DIM
    H = DN_HEADS
    HC = H * C

    @pl.when(s == 0)
    def _():
        state_ref[...] = jnp.zeros_like(state_ref)

    tile = qkv_ref[...]
    prev = jnp.where(s == 0, 0.0, prev_ref[...])
    xf = jnp.concatenate([prev, tile], axis=0)
    cw = convw_ref[...]
    acc = tile * cw[DN_CONV - 1:DN_CONV]
    for sft in range(1, DN_CONV):
        shifted = pltpu.roll(xf, sft, 0)[SUBLANES:SUBLANES + ts]
        acc = acc + shifted * cw[DN_CONV - 1 - sft:DN_CONV - sft]
    act = _silu(acc)

    ab = ab_ref[...]
    neg_a = -jnp.exp(alog_ref[...])
    xs = ab + dtb_ref[...]
    softplus = jnp.maximum(xs, 0.0) + jnp.log1p(jnp.exp(-jnp.abs(xs)))
    g_all = neg_a * softplus
    beta_all = _sigmoid(ab)

    zg = _silu(z_ref[...])
    dng = dng_ref[...]

    ri = lax.broadcasted_iota(jnp.int32, (HC, HC), 0)
    ci = lax.broadcasted_iota(jnp.int32, (HC, HC), 1)
    log2_c = C.bit_length() - 1
    same_head = (ri >> log2_c) == (ci >> log2_c)
    causal = same_head & (ri >= ci)
    strict = same_head & (ri > ci)
    eye = jnp.where(ri == ci, 1.0, 0.0).astype(F32)
    r64 = lax.broadcasted_iota(jnp.int32, (C, C), 0)
    c64 = lax.broadcasted_iota(jnp.int32, (C, C), 1)
    tril = jnp.where(r64 >= c64, 1.0, 0.0).astype(F32)

    def chunk_prep(chunks, out):
        gcum = [_dot_f32(tril, g_all[c * C:(c + 1) * C]) for c in chunks]
        yield
        gc_col = [_stack_heads(g, 0, 0, 1) for g in gcum]
        gc_row = [jnp.concatenate([g.T[h:h + 1, :] for h in range(H)], axis=1) for g in gcum]
        g_last = [jnp.concatenate([jnp.broadcast_to(g[C - 1:C, h:h + 1], (C, 1)) for h in range(H)], axis=0)
                  for g in gcum]
        yield
        beta = [_stack_heads(beta_all, c * C, H, 1) for c in chunks]
        q = [_stack_heads(act, c * C, 0, hd) for c in chunks]
        k = [_stack_heads(act, c * C, DN_WIDTH, hd) for c in chunks]
        v = [_stack_heads(act, c * C, 2 * DN_WIDTH, hd) for c in chunks]
        q = [x * lax.rsqrt(jnp.sum(x * x, axis=-1, keepdims=True) + EPS) * (hd ** -0.5) for x in q]
        yield
        k = [x * lax.rsqrt(jnp.sum(x * x, axis=-1, keepdims=True) + EPS) for x in k]
        yield
        decay = [jnp.where(causal, jnp.exp(jnp.where(causal, gc - gr, 0.0)), 0.0)
                 for gc, gr in zip(gc_col, gc_row)]
        yield
        eg = [jnp.exp(gc) for gc in gc_col]
        kb = [x * b for x, b in zip(k, beta)]
        k16 = [x.astype(BF16) for x in k]
        lower = [jnp.where(strict, _dot_nt(a.astype(BF16), b) * d, 0.0) for a, b, d in zip(kb, k16, decay)]
        yield
        t = yield from _unit_lower_inverses(lower, eye, ri, ci)
        t16 = [x.astype(BF16) for x in t]
        yield
        uw = [_dot(t, jnp.concatenate([x * b, y * e], axis=1).astype(BF16))
              for t, x, b, y, e in zip(t16, v, beta, kb, eg)]
        yield
        attn = [(_dot_nt(x.astype(BF16), y) * d).astype(BF16) for x, y, d in zip(q, k16, decay)]
        yield
        for n, c in enumerate(chunks):
            out[c] = dict(u=uw[n][:, :hd], w=uw[n][:, hd:].astype(BF16), attn=attn[n],
                          qd=(q[n] * eg[n]).astype(BF16),
                          kd=(k[n] * jnp.exp(g_last[n] - gc_col[n])).astype(BF16),
                          el=[jnp.exp(gcum[n][C - 1:C, h:h + 1]) for h in range(H)])

    def recurrence(chunks, prepared):
        for c in chunks:
            pc = prepared[c]
            r0 = c * C
            v_new = []
            q_state = []
            for h in range(H):
                rows = slice(h * C, (h + 1) * C)
                st16 = state_ref[h].astype(BF16)
                both = _dot(jnp.concatenate([pc["w"][rows], pc["qd"][rows]], axis=0), st16)
                v_new.append(pc["u"][rows] - both[:C])
                q_state.append(both[C:])
            yield
            vn16 = jnp.concatenate(v_new, axis=0).astype(BF16)
            intra = _dot(pc["attn"], vn16)
            for h in range(H):
                rows = slice(h * C, (h + 1) * C)
                state_ref[h] = state_ref[h] * pc["el"][h] + _dot_tn(pc["kd"][rows], vn16[rows])
            yield
            for h in range(H):
                rows = slice(h * C, (h + 1) * C)
                o = q_state[h] + intra[rows]
                y = _rms_scale(o) * dng * zg[r0:r0 + C, h * hd:(h + 1) * hd]
                ya_ref[r0:r0 + C, h * hd:(h + 1) * hd] = y.astype(ya_ref.dtype)
            yield

    n_chunks = ts // C
    groups = [list(range(g0, min(g0 + DN_GROUP, n_chunks))) for g0 in range(0, n_chunks, DN_GROUP)]
    prepared = {}
    pending = iter(())
    for grp in groups:
        n = 0
        for _ in chunk_prep(grp, prepared):
            n += 1
            if n % DN_WEAVE == 0:
                next(pending, None)
        for _ in pending:
            pass
        pending = recurrence(grp, prepared)
    for _ in pending:
        pass


def _deltanet(qkv, z, ab, conv_w, a_log, dt_bias, dn_norm_g, batch, seq, ts):
    N = qkv.shape[0]
    per_seq = seq // ts
    n_qkv = qkv.shape[1]
    pad = LANES - DN_HEADS
    alog_row = jnp.pad(a_log, (0, pad)).reshape(1, LANES)
    dtb_row = jnp.pad(dt_bias, (0, pad)).reshape(1, LANES)

    def row(b, s):
        return (b * per_seq + s, 0)

    def prev(b, s):
        return (jnp.maximum((b * per_seq + s) * (ts // SUBLANES) - 1, 0), 0)

    def const(b, s):
        return (0, 0)

    return pl.pallas_call(
        _deltanet_kernel,
        out_shape=jax.ShapeDtypeStruct((N, DN_WIDTH), BF16),
        grid=(batch, per_seq),
        in_specs=[pl.BlockSpec((ts, n_qkv), row),
                  pl.BlockSpec((SUBLANES, n_qkv), prev),
                  pl.BlockSpec((ts, DN_WIDTH), row),
                  pl.BlockSpec((ts, LANES), row),
                  pl.BlockSpec(conv_w.shape, const),
                  pl.BlockSpec((1, LANES), const),
                  pl.BlockSpec((1, LANES), const),
                  pl.BlockSpec((1, DN_HEAD_DIM), const)],
        out_specs=pl.BlockSpec((ts, DN_WIDTH), row),
        scratch_shapes=[pltpu.VMEM((DN_HEADS, DN_HEAD_DIM, DN_HEAD_DIM), F32)],
        compiler_params=pltpu.CompilerParams(dimension_semantics=("arbitrary", "arbitrary"),
                                             vmem_limit_bytes=VMEM_LIMIT),
        name="deltanet",
    )(qkv, qkv, z, ab, conv_w, alog_row, dtb_row, dn_norm_g.reshape(1, DN_HEAD_DIM))


def _mix_kernel(ya_ref, pu_ref, pprev_ref, ga_ref, gb_ref, x_ref, gt1_ref, sc2_ref, sh2_ref, g2_ref,
                poolw_ref, pscale_ref, wla_ref, wlb_ref, wout_ref, wr_ref, br_ref,
                h_ref, xn_ref, route_ref, counts_ref, carry_ref, *, per_seq):
    i = pl.program_id(0)
    tm = x_ref.shape[0]
    first = (i % per_seq) == 0

    @pl.when(i == 0)
    def _():
        carry_ref[...] = jnp.zeros_like(carry_ref)

    pu = pu_ref[...]
    hist = jnp.where(first, 0.0, pprev_ref[...])
    pf = jnp.concatenate([hist, pu], axis=0)
    pos = (i % per_seq) * tm + lax.broadcasted_iota(jnp.int32, (tm, 1), 0)
    pscale = pscale_ref[...]
    yb_parts = []
    for gi, win in enumerate(POOL_WINDOWS):
        lo = gi * POOL_GROUP_DIM
        xg = pf[:, lo:lo + POOL_GROUP_DIM]
        acc = xg
        span = 1
        while span < win:
            acc = acc + pltpu.roll(acc, span, 0)
            span *= 2
        cnt = jnp.minimum(pos + 1, win).astype(F32)
        pooled = acc[POOL_HIST:] / cnt - pu[:, lo:lo + POOL_GROUP_DIM]
        yg = _dot(pooled.astype(BF16), poolw_ref[gi])
        yb_parts.append(yg * pscale[:, lo:lo + POOL_GROUP_DIM])
    yb = jnp.concatenate(yb_parts, axis=1)

    lift_a = _dot(ya_ref[...], wla_ref[...])
    lift_b = _dot(yb.astype(BF16), wlb_ref[...])
    mixed = _sigmoid(ga_ref[...]) * lift_a + _sigmoid(gb_ref[...]) * lift_b
    h = x_ref[...] + gt1_ref[0] * _dot(mixed.astype(BF16), wout_ref[...])
    h_ref[...] = h
    xn = _rms_scale(h) * g2_ref[...] * (1.0 + sc2_ref[0]) + sh2_ref[0]
    xn_ref[...] = xn

    xn_hi = xn.astype(BF16)
    xn_lo = (xn - xn_hi.astype(F32)).astype(BF16)
    wr = wr_ref[...]
    wr_hi = wr.astype(BF16)
    wr_lo = (wr - wr_hi.astype(F32)).astype(BF16)
    logits = _dot(jnp.concatenate([xn_hi, xn_lo, xn_hi], axis=1),
                  jnp.concatenate([wr_hi, wr_hi, wr_lo], axis=0)) + br_ref[...]
    li = lax.broadcasted_iota(jnp.int32, (tm, LANES), 1).astype(F32)
    neg = -jnp.inf
    big = float(LANES)
    is_g = li < MOE_GROUPS
    lg = jnp.where(is_g, logits, neg)
    mg = jnp.max(lg, axis=-1, keepdims=True)
    p_grp = 1.0 / jnp.sum(jnp.where(is_g, jnp.exp(lg - mg), 0.0), axis=-1, keepdims=True)
    grp = jnp.min(jnp.where(lg == mg, li, big), axis=-1, keepdims=True)
    e_lo = MOE_GROUPS + MOE_EXPERTS_PER_GROUP * grp
    is_e = (li >= e_lo) & (li < e_lo + MOE_EXPERTS_PER_GROUP)
    le = jnp.where(is_e, logits, neg)
    m1 = jnp.max(le, axis=-1, keepdims=True)
    i1 = jnp.min(jnp.where(le == m1, li, big), axis=-1, keepdims=True)
    le2 = jnp.where(li == i1, neg, le)
    m2 = jnp.max(le2, axis=-1, keepdims=True)
    i2 = jnp.min(jnp.where(le2 == m2, li, big), axis=-1, keepdims=True)
    e2 = jnp.exp(m2 - m1)
    w1 = p_grp / (1.0 + e2)
    w2 = p_grp * e2 / (1.0 + e2)
    eid1 = i1 - MOE_GROUPS
    eid2 = i2 - MOE_GROUPS

    oh1 = li == eid1
    oh2 = li == eid2
    onehot = jnp.where(oh1 | oh2, 1.0, 0.0)
    ri = lax.broadcasted_iota(jnp.int32, (tm, tm), 0)
    ci = lax.broadcasted_iota(jnp.int32, (tm, tm), 1)
    below = jnp.where(ri > ci, 1.0, 0.0).astype(BF16)
    carry = carry_ref[0:1, :]
    before = _dot(below, onehot.astype(BF16)) + carry
    rank1 = jnp.sum(jnp.where(oh1, before, 0.0), axis=-1, keepdims=True)
    rank2 = jnp.sum(jnp.where(oh2, before, 0.0), axis=-1, keepdims=True)
    new_carry = carry + jnp.sum(onehot, axis=0, keepdims=True)
    carry_ref[...] = jnp.broadcast_to(new_carry, carry_ref.shape)
    counts_ref[...] = jnp.broadcast_to(new_carry, counts_ref.shape)

    route = jnp.where(li == 0.0, w1, 0.0)
    for lane, val in ((1, w2), (2, eid1), (3, eid2), (4, rank1), (5, rank2)):
        route = jnp.where(li == float(lane), val, route)
    route_ref[...] = route


def _mix(ya, pu, ga, gb, x2, gt1, sc2, sh2, g2, pool_w, pool_scale, w_lift_a, w_lift_b, w_out,
         w_router, b_router, seq, tm):
    N, D = x2.shape
    per_seq = seq // tm

    def row(i):
        return (i, 0)

    def prev(i):
        return (jnp.maximum(i * (tm // POOL_HIST) - 1, 0), 0)

    def mod(i):
        return (i // per_seq, 0, 0)

    def const(i):
        return (0, 0)

    return pl.pallas_call(
        functools.partial(_mix_kernel, per_seq=per_seq),
        out_shape=[jax.ShapeDtypeStruct((N, D), F32),
                   jax.ShapeDtypeStruct((N, D), F32),
                   jax.ShapeDtypeStruct((N, LANES), F32),
                   jax.ShapeDtypeStruct((SUBLANES, LANES), F32)],
        grid=(N // tm,),
        in_specs=[pl.BlockSpec((tm, DN_WIDTH), row),
                  pl.BlockSpec((tm, POOL_WIDTH), row),
                  pl.BlockSpec((POOL_HIST, POOL_WIDTH), prev),
                  pl.BlockSpec((tm, D), row),
                  pl.BlockSpec((tm, D), row),
                  pl.BlockSpec((tm, D), row),
                  pl.BlockSpec((1, 1, D), mod),
                  pl.BlockSpec((1, 1, D), mod),
                  pl.BlockSpec((1, 1, D), mod),
                  pl.BlockSpec((1, D), const),
                  pl.BlockSpec(pool_w.shape, lambda i: (0, 0, 0)),
                  pl.BlockSpec((1, POOL_WIDTH), const),
                  pl.BlockSpec(w_lift_a.shape, const),
                  pl.BlockSpec(w_lift_b.shape, const),
                  pl.BlockSpec(w_out.shape, const),
                  pl.BlockSpec(w_router.shape, const),
                  pl.BlockSpec((1, LANES), const)],
        out_specs=[pl.BlockSpec((tm, D), row),
                   pl.BlockSpec((tm, D), row),
                   pl.BlockSpec((tm, LANES), row),
                   pl.BlockSpec((SUBLANES, LANES), const)],
        scratch_shapes=[pltpu.VMEM((SUBLANES, LANES), F32)],
        compiler_params=pltpu.CompilerParams(dimension_semantics=("arbitrary",),
                                             vmem_limit_bytes=VMEM_LIMIT),
        name="mix",
    )(ya, pu, pu, ga, gb, x2, gt1, sc2, sh2, g2, pool_w, pool_scale, w_lift_a, w_lift_b, w_out,
      w_router, b_router)


def _dest_kernel(route_ref, pstart_ref, dest_ref):
    route = route_ref[...]
    tm = route.shape[0]
    li = lax.broadcasted_iota(jnp.int32, (tm, LANES), 1).astype(F32)
    pstart = pstart_ref[...]
    out = jnp.zeros((tm, LANES), F32)
    for k in range(2):
        eid = route[:, 2 + k:3 + k]
        rank = route[:, 4 + k:5 + k]
        slot = jnp.sum(jnp.where(li == eid, pstart, 0.0), axis=-1, keepdims=True) + rank
        out = jnp.where(li == float(k), slot, out)
    dest_ref[...] = out.astype(jnp.int32)


def _dest(route, pstart_row, tm):
    N = route.shape[0]
    return pl.pallas_call(
        _dest_kernel,
        out_shape=jax.ShapeDtypeStruct((N, LANES), jnp.int32),
        grid=(N // tm,),
        in_specs=[pl.BlockSpec((tm, LANES), lambda i: (i, 0)),
                  pl.BlockSpec((1, LANES), lambda i: (0, 0))],
        out_specs=pl.BlockSpec((tm, LANES), lambda i: (i, 0)),
        compiler_params=pltpu.CompilerParams(dimension_semantics=("arbitrary",)),
        name="dest",
    )(route, pstart_row)


def _dispatch_kernel(pend_ref, padded_ref, nused_ref, d0_ref, d1_ref, x_ref, xs_ref, zero_ref, sem, zsem):
    tm = x_ref.shape[0]
    nb = xs_ref.shape[0] // MOE_ROWS

    @pl.when(pl.program_id(0) == 0)
    def _():
        zero_ref[...] = jnp.zeros_like(zero_ref)

        def zero_block(row0):
            return pltpu.make_async_copy(zero_ref, xs_ref.at[pl.ds(row0, MOE_ROWS)], zsem)

        def tail_block(e):
            return zero_block(pl.multiple_of(pend_ref[e] - MOE_ROWS, MOE_ROWS))

        def unused_block(j):
            return zero_block(pl.multiple_of(j * MOE_ROWS, MOE_ROWS))

        def for_each_zero_block(act):
            def per_expert(e, carry):
                @pl.when(padded_ref[e] > 0)
                def _():
                    act(tail_block(e))
                return carry

            def per_unused(j, carry):
                act(unused_block(j))
                return carry

            lax.fori_loop(0, MOE_EXPERTS, per_expert, 0)
            lax.fori_loop(nused_ref[0], nb, per_unused, 0)

        for_each_zero_block(lambda cp: cp.start())
        for_each_zero_block(lambda cp: cp.wait())

    def row_copy(t, d):
        return pltpu.make_async_copy(x_ref.at[pl.ds(t, 1)], xs_ref.at[pl.ds(d, 1)], sem)

    def issue(t, carry):
        row_copy(t, d0_ref[0, t]).start()
        row_copy(t, d1_ref[0, t]).start()
        return carry

    lax.fori_loop(0, tm, issue, 0, unroll=8)

    def drain(t, carry):
        row_copy(t, d0_ref[0, t]).wait()
        row_copy(t, d1_ref[0, t]).wait()
        return carry

    lax.fori_loop(0, tm, drain, 0, unroll=8)


def _dispatch(pend, padded, nused, d0, d1, xn, n_slots, tm):
    N, D = xn.shape
    nt = N // tm
    smem = functools.partial(pl.BlockSpec, (None, 1, tm), lambda i, *_: (i, 0, 0),
                             memory_space=pltpu.SMEM)
    return pl.pallas_call(
        _dispatch_kernel,
        out_shape=jax.ShapeDtypeStruct((n_slots, D), xn.dtype),
        grid_spec=pltpu.PrefetchScalarGridSpec(
            num_scalar_prefetch=3,
            grid=(nt,),
            in_specs=[smem(), smem(), pl.BlockSpec((tm, D), lambda i, *_: (i, 0))],
            out_specs=pl.BlockSpec(memory_space=pl.ANY),
            scratch_shapes=[pltpu.VMEM((MOE_ROWS, D), xn.dtype), pltpu.SemaphoreType.DMA,
                            pltpu.SemaphoreType.DMA]),
        compiler_params=pltpu.CompilerParams(dimension_semantics=("arbitrary",),
                                             vmem_limit_bytes=VMEM_LIMIT),
        name="dispatch",
    )(pend, padded, nused, d0.reshape(nt, 1, tm), d1.reshape(nt, 1, tm), xn)


def _experts_kernel(blk_e_ref, nused_ref, x_ref, wg_ref, wu_ref, wd_ref, y_ref):
    del blk_e_ref
    used = pl.program_id(0) < nused_ref[0]

    @pl.when(used)
    def _():
        x = x_ref[...].astype(BF16)
        hid = _silu(_dot(x, wg_ref[...].astype(BF16))) * _dot(x, wu_ref[...].astype(BF16))
        y_ref[...] = _dot(hid.astype(BF16), wd_ref[...].astype(BF16))

    @pl.when(jnp.logical_not(used))
    def _():
        y_ref[...] = jnp.zeros_like(y_ref)


def _experts(blk_e, nused, xs, w_gate, w_up, w_down):
    P, D = xs.shape
    nb = P // MOE_ROWS
    dff = w_gate.shape[2]

    def rows(j, blk_e_ref, nused_ref):
        return (jnp.maximum(jnp.minimum(j, nused_ref[0] - 1), 0), 0)

    def wsel(j, blk_e_ref, nused_ref):
        return (blk_e_ref[j], 0, 0)

    return pl.pallas_call(
        _experts_kernel,
        out_shape=jax.ShapeDtypeStruct((P, D), F32),
        grid_spec=pltpu.PrefetchScalarGridSpec(
            num_scalar_prefetch=2,
            grid=(nb,),
            in_specs=[pl.BlockSpec((MOE_ROWS, D), rows),
                      pl.BlockSpec((None, D, dff), wsel),
                      pl.BlockSpec((None, D, dff), wsel),
                      pl.BlockSpec((None, dff, D), wsel)],
            out_specs=pl.BlockSpec((MOE_ROWS, D), lambda j, blk_e_ref, nused_ref: (j, 0))),
        compiler_params=pltpu.CompilerParams(dimension_semantics=("arbitrary",),
                                             vmem_limit_bytes=VMEM_LIMIT),
        name="experts",
    )(blk_e, nused, xs, w_gate, w_up, w_down)


def _combine_kernel(d0_ref, d1_ref, d0n_ref, d1n_ref, route_ref, h_ref, gt2_ref, gf_ref, y_ref, o_ref,
                    buf0_ref, buf1_ref, sems):
    i = pl.program_id(0)
    tm = h_ref.shape[0]
    slot = i % 2

    def row_copy(buf_ref, sl, t, d):
        return pltpu.make_async_copy(y_ref.at[pl.ds(d, 1)], buf_ref.at[sl, pl.ds(t, 1)], sems.at[sl])

    def gather(da_ref, db_ref, sl):
        def issue(t, carry):
            row_copy(buf0_ref, sl, t, da_ref[0, t]).start()
            row_copy(buf1_ref, sl, t, db_ref[0, t]).start()
            return carry

        lax.fori_loop(0, tm, issue, 0, unroll=8)

    @pl.when(i == 0)
    def _():
        gather(d0_ref, d1_ref, slot)

    @pl.when(i + 1 < pl.num_programs(0))
    def _():
        gather(d0n_ref, d1n_ref, 1 - slot)

    def drain(t, carry):
        row_copy(buf0_ref, slot, t, d0_ref[0, t]).wait()
        row_copy(buf1_ref, slot, t, d1_ref[0, t]).wait()
        return carry

    lax.fori_loop(0, tm, drain, 0, unroll=8)

    route = route_ref[...]
    moe = buf0_ref[slot] * route[:, 0:1] + buf1_ref[slot] * route[:, 1:2]
    h = h_ref[...] + gt2_ref[0] * moe
    o_ref[...] = _rms_scale(h) * gf_ref[...]


def _combine(d0, d1, route, h, gt2, gf, y, seq, tm):
    N, D = h.shape
    nt = N // tm
    per_seq = seq // tm
    cur = functools.partial(pl.BlockSpec, (None, 1, tm), lambda i: (i, 0, 0), memory_space=pltpu.SMEM)
    nxt = functools.partial(pl.BlockSpec, (None, 1, tm), lambda i: (jnp.minimum(i + 1, nt - 1), 0, 0),
                            memory_space=pltpu.SMEM)
    d0 = d0.reshape(nt, 1, tm)
    d1 = d1.reshape(nt, 1, tm)
    return pl.pallas_call(
        _combine_kernel,
        out_shape=jax.ShapeDtypeStruct((N, D), F32),
        grid=(nt,),
        in_specs=[cur(), cur(), nxt(), nxt(),
                  pl.BlockSpec((tm, LANES), lambda i: (i, 0)),
                  pl.BlockSpec((tm, D), lambda i: (i, 0)),
                  pl.BlockSpec((1, 1, D), lambda i: (i // per_seq, 0, 0)),
                  pl.BlockSpec((1, D), lambda i: (0, 0)),
                  pl.BlockSpec(memory_space=pl.ANY)],
        out_specs=pl.BlockSpec((tm, D), lambda i: (i, 0)),
        scratch_shapes=[pltpu.VMEM((2, tm, D), F32), pltpu.VMEM((2, tm, D), F32),
                        pltpu.SemaphoreType.DMA((2,))],
        compiler_params=pltpu.CompilerParams(dimension_semantics=("arbitrary",),
                                             vmem_limit_bytes=VMEM_LIMIT),
        name="combine",
    )(d0, d1, d0, d1, route, h, gt2, gf, y)


def _layer(h2, mod, batch, seq, norm1_g, w_in, conv_w, a_log, dt_bias, dn_norm_g, pool_w, pool_scale,
           w_lift_a, w_lift_b, w_out, norm2_g, w_rg, b_rg, w_re, b_re, w_gate, w_up, w_down, final_g):
    N, D = h2.shape
    sh1, sc1, gt1, sh2, sc2, gt2 = [m.reshape(batch, 1, D) for m in jnp.split(mod, 6, axis=-1)]

    o_z = 3 * DN_WIDTH
    o_a = o_z + DN_WIDTH
    o_pu = o_a + 2 * DN_HEADS
    o_g = o_pu + POOL_WIDTH
    w_qkvz = w_in[:, :o_a].astype(BF16)
    w_ab = jnp.pad(w_in[:, o_a:o_pu], ((0, 0), (0, LANES - 2 * DN_HEADS))).astype(BF16)
    w_pu = w_in[:, o_pu:o_g].astype(BF16)
    w_g = w_in[:, o_g:].astype(BF16)

    tm = min(256, seq)
    qkv, z, ab, pu, ga, gb = _inproj(h2, norm1_g.reshape(1, D), sc1, sh1, w_qkvz, w_ab, w_pu, w_g, seq,
                                     min(512, seq))

    ts = min(512, seq)
    ya = _deltanet(qkv, z, ab, conv_w, a_log, dt_bias, dn_norm_g, batch, seq, ts)

    n_r = MOE_GROUPS + MOE_EXPERTS
    w_router = jnp.pad(jnp.concatenate([w_rg, w_re], axis=1), ((0, 0), (0, LANES - n_r)))
    b_router = jnp.pad(jnp.concatenate([b_rg, b_re]), (0, LANES - n_r)).reshape(1, LANES)
    h_mid, xn2, route, counts = _mix(
        ya, pu, ga, gb, h2, gt1, sc2, sh2, norm2_g.reshape(1, D), pool_w.astype(BF16),
        pool_scale.reshape(1, POOL_WIDTH), w_lift_a.astype(BF16), w_lift_b.astype(BF16),
        w_out.astype(BF16), w_router, b_router, seq, tm)

    cnt = counts[0, :MOE_EXPERTS].astype(jnp.int32)
    padded = (cnt + MOE_ROWS - 1) // MOE_ROWS * MOE_ROWS
    pend = jnp.cumsum(padded)
    pstart = pend - padded
    n_slots = 2 * N + MOE_EXPERTS * MOE_ROWS
    nb = n_slots // MOE_ROWS
    nused = (pend[-1] // MOE_ROWS).astype(jnp.int32).reshape(1)
    blk_row0 = jnp.arange(nb, dtype=jnp.int32) * MOE_ROWS
    blk_e = jnp.minimum(jnp.sum((pend[None, :] <= blk_row0[:, None]).astype(jnp.int32), axis=1),
                        MOE_EXPERTS - 1)
    pstart_row = jnp.pad(pstart.astype(F32), (0, LANES - MOE_EXPERTS)).reshape(1, LANES)

    dest = _dest(route, pstart_row, min(2048, N))
    d0 = dest[:, 0]
    d1 = dest[:, 1]
    xs = _dispatch(pend.astype(jnp.int32), padded, nused, d0, d1, xn2, n_slots, min(1024, N))
    y = _experts(blk_e, nused, xs, w_gate, w_up, w_down)
    return _combine(d0, d1, route, h_mid, gt2, final_g.reshape(1, D), y, seq, min(512, seq))


def kernel(x, c, w_ada, b_ada, norm1_g, w_in, conv_w, a_log, dt_bias, dn_norm_g, pool_w, pool_scale,
           w_lift_a, w_lift_b, w_out, norm2_g, w_router_group, b_router_group, w_router_expert,
           b_router_expert, w_gate, w_up, w_down, final_norm_g):
    batch, seq, D = x.shape
    depth = w_ada.shape[0]
    assert depth == 1, "the combine kernel fuses the final rmsnorm, so exactly one layer is supported"
    assert seq % DN_CHUNK == 0 and D % LANES == 0
    h2 = x.reshape(batch * seq, D)
    l = 0
    mod = _ada(c, w_ada[l], b_ada[l])
    out = _layer(h2, mod, batch, seq, norm1_g[l], w_in[l], conv_w[l], a_log[l], dt_bias[l], dn_norm_g[l],
                 pool_w[l], pool_scale[l], w_lift_a[l], w_lift_b[l], w_out[l], norm2_g[l],
                 w_router_group[l], b_router_group[l], w_router_expert[l], b_router_expert[l],
                 w_gate[l], w_up[l], w_down[l], final_norm_g)
    return out.reshape(batch, seq, D)
```

```python
import functools

import jax
import jax.numpy as jnp
from jax import lax
from jax.experimental import pallas as pl
from jax.experimental.pallas import tpu as pltpu

F32 = jnp.float32
BF16 = jnp.bfloat16
HIGHEST = lax.Precision.HIGHEST

EPS = 1e-6
LANES = 128
SUBLANES = 8
DN_HEADS = 4
DN_HEAD_DIM = 128
DN_WIDTH = DN_HEADS * DN_HEAD_DIM
DN_CONV = 4
DN_CHUNK = 64
DN_GROUP = 4
DN_WEAVE = 2
POOL_WINDOWS = (2, 4, 8, 16)
POOL_GROUP_DIM = 128
POOL_WIDTH = POOL_GROUP_DIM * len(POOL_WINDOWS)
POOL_HIST = 16
MOE_GROUPS = 4
MOE_EXPERTS_PER_GROUP = 8
MOE_EXPERTS = MOE_GROUPS * MOE_EXPERTS_PER_GROUP
MOE_ROWS = 512
INPROJ_SUB = 256
VMEM_LIMIT = 56 * 1024 * 1024


def _sigmoid(x):
    return 1.0 / (1.0 + jnp.exp(-x))


def _silu(x):
    return x * _sigmoid(x)


def _dot(a, b):
    return jnp.dot(a, b, preferred_element_type=F32)


def _dot_f32(a, b):
    return jnp.dot(a, b, preferred_element_type=F32, precision=HIGHEST)


def _dot_nt(a, b):
    return lax.dot_general(a, b, (((1,), (1,)), ((), ())), preferred_element_type=F32)


def _dot_tn(a, b):
    return lax.dot_general(a, b, (((0,), (0,)), ((), ())), preferred_element_type=F32)


def _split_dot(a, b):
    a_hi = a.astype(BF16)
    a_lo = (a - a_hi.astype(F32)).astype(BF16)
    b_hi = b.astype(BF16)
    b_lo = (b - b_hi.astype(F32)).astype(BF16)
    return _dot(a_hi, b_hi) + (_dot(a_hi, b_lo) + _dot(a_lo, b_hi))


def _rms_scale(x):
    return x * lax.rsqrt(jnp.mean(x * x, axis=-1, keepdims=True) + EPS)


def _ada_kernel(c_ref, w_ref, b_ref, o_ref):
    o_ref[...] = _dot_f32(_silu(c_ref[...]), w_ref[...]) + b_ref[...]


def _ada(c, w_ada, b_ada):
    B, D = c.shape
    n_out = w_ada.shape[1]
    tn = D
    return pl.pallas_call(
        _ada_kernel,
        out_shape=jax.ShapeDtypeStruct((B, n_out), F32),
        grid=(n_out // tn,),
        in_specs=[pl.BlockSpec((B, D), lambda j: (0, 0)),
                  pl.BlockSpec((D, tn), lambda j: (0, j)),
                  pl.BlockSpec((1, tn), lambda j: (0, j))],
        out_specs=pl.BlockSpec((B, tn), lambda j: (0, j)),
        compiler_params=pltpu.CompilerParams(dimension_semantics=("arbitrary",),
                                             vmem_limit_bytes=VMEM_LIMIT),
        name="ada",
    )(c, w_ada, b_ada.reshape(1, n_out))


def _inproj_kernel(x_ref, g_ref, sc_ref, sh_ref, wqkvz_ref, wab_ref, wpu_ref, wg_ref,
                   qkv_ref, z_ref, ab_ref, pu_ref, ga_ref, gb_ref):
    n_qkv = qkv_ref.shape[1]
    d = ga_ref.shape[1]
    tm = x_ref.shape[0]
    sub = min(INPROJ_SUB, tm)
    for r0 in range(0, tm, sub):
        rows = slice(r0, r0 + sub)
        x = x_ref[rows, :]
        xn = _rms_scale(x) * g_ref[...] * (1.0 + sc_ref[0]) + sh_ref[0]
        xb = xn.astype(BF16)
        qkvz = _dot(xb, wqkvz_ref[...])
        qkv_ref[rows, :] = qkvz[:, :n_qkv]
        z_ref[rows, :] = qkvz[:, n_qkv:]
        ab_ref[rows, :] = _dot(xb, wab_ref[...])
        pu_ref[rows, :] = _dot(xb, wpu_ref[...])
        gates = _dot(xb, wg_ref[...])
        ga_ref[rows, :] = gates[:, :d]
        gb_ref[rows, :] = gates[:, d:]


def _inproj(x2, g, sc, sh, w_qkvz, w_ab, w_pu, w_g, seq, tm):
    N, D = x2.shape
    per_seq = seq // tm
    n_qkv = 3 * DN_WIDTH

    def row(i):
        return (i, 0)

    def mod(i):
        return (i // per_seq, 0, 0)

    def const(i):
        return (0, 0)

    outs = [(N, n_qkv), (N, DN_WIDTH), (N, LANES), (N, POOL_WIDTH), (N, D), (N, D)]
    return pl.pallas_call(
        _inproj_kernel,
        out_shape=[jax.ShapeDtypeStruct(s, F32) for s in outs],
        grid=(N // tm,),
        in_specs=[pl.BlockSpec((tm, D), row),
                  pl.BlockSpec((1, D), const),
                  pl.BlockSpec((1, 1, D), mod),
                  pl.BlockSpec((1, 1, D), mod),
                  pl.BlockSpec(w_qkvz.shape, const),
                  pl.BlockSpec(w_ab.shape, const),
                  pl.BlockSpec(w_pu.shape, const),
                  pl.BlockSpec(w_g.shape, const)],
        out_specs=[pl.BlockSpec((tm, s[1]), row) for s in outs],
        compiler_params=pltpu.CompilerParams(dimension_semantics=("arbitrary",),
                                             vmem_limit_bytes=VMEM_LIMIT),
        name="inproj",
    )(x2, g, sc, sh, w_qkvz, w_ab, w_pu, w_g)


def _unit_lower_inverses(lowers, eye, ri, ci):
    def mm(a, b):
        return _dot(a.astype(BF16), b.astype(BF16))

    base = 3
    same = (ri >> base) == (ci >> base)
    ld = [jnp.where(same, l, 0.0) for l in lowers]
    a0 = [eye - x for x in ld]
    p1 = [mm(x, x) for x in ld]
    yield
    x1 = [a + mm(a, p) for a, p in zip(a0, p1)]
    yield
    p2 = [mm(p, p) for p in p1]
    yield
    t = [x + mm(x, p) for x, p in zip(x1, p2)]
    yield
    sh = base
    while (1 << sh) < DN_CHUNK:
        merge = ((ri >> (sh + 1)) == (ci >> (sh + 1))) & ((ri >> sh) != (ci >> sh))
        y = [mm(jnp.where(merge, l, 0.0), x) for l, x in zip(lowers, t)]
        yield
        t = [x - mm(x, yy) for x, yy in zip(t, y)]
        yield
        sh += 1
    resid = [(eye - x) - _split_dot(l, x) for l, x in zip(lowers, t)]
    yield
    return [x + mm(x, r) for x, r in zip(t, resid)]


def _stack_heads(x, r0, c0, width):
    return jnp.concatenate(
        [x[r0:r0 + DN_CHUNK, c0 + h * width:c0 + (h + 1) * width] for h in range(DN_HEADS)], axis=0)


def _deltanet_kernel(qkv_ref, prev_ref, z_ref, ab_ref, convw_ref, alog_ref, dtb_ref, dng_ref,
                     ya_ref, state_ref):
    s = pl.program_id(1)
    ts = qkv_ref.shape[0]
    C = DN_CHUNK
    hd = DN_HEAD_DIM
    H = DN_HEADS
    HC = H * C

    @pl.when(s == 0)
    def _():
        state_ref[...] = jnp.zeros_like(state_ref)

    tile = qkv_ref[...]
    prev = jnp.where(s == 0, 0.0, prev_ref[...])
    xf = jnp.concatenate([prev, tile], axis=0)
    cw = convw_ref[...]
    acc = tile * cw[DN_CONV - 1:DN_CONV]
    for sft in range(1, DN_CONV):
        shifted = pltpu.roll(xf, sft, 0)[SUBLANES:SUBLANES + ts]
        acc = acc + shifted * cw[DN_CONV - 1 - sft:DN_CONV - sft]
    act = _silu(acc)

    ab = ab_ref[...]
    neg_a = -jnp.exp(alog_ref[...])
    xs = ab + dtb_ref[...]
    softplus = jnp.maximum(xs, 0.0) + jnp.log1p(jnp.exp(-jnp.abs(xs)))
    g_all = neg_a * softplus
    beta_all = _sigmoid(ab)

    zg = _silu(z_ref[...])
    dng = dng_ref[...]

    ri = lax.broadcasted_iota(jnp.int32, (HC, HC), 0)
    ci = lax.broadcasted_iota(jnp.int32, (HC, HC), 1)
    log2_c = C.bit_length() - 1
    same_head = (ri >> log2_c) == (ci >> log2_c)
    causal = same_head & (ri >= ci)
    strict = same_head & (ri > ci)
    eye = jnp.where(ri == ci, 1.0, 0.0).astype(F32)
    r64 = lax.broadcasted_iota(jnp.int32, (C, C), 0)
    c64 = lax.broadcasted_iota(jnp.int32, (C, C), 1)
    tril = jnp.where(r64 >= c64, 1.0, 0.0).astype(F32)

    def chunk_prep(chunks, out):
        gcum = [_dot_f32(tril, g_all[c * C:(c + 1) * C]) for c in chunks]
        yield
        gc_col = [_stack_heads(g, 0, 0, 1) for g in gcum]
        gc_row = [jnp.concatenate([g.T[h:h + 1, :] for h in range(H)], axis=1) for g in gcum]
        g_last = [jnp.concatenate([jnp.broadcast_to(g[C - 1:C, h:h + 1], (C, 1)) for h in range(H)], axis=0)
                  for g in gcum]
        yield
        beta = [_stack_heads(beta_all, c * C, H, 1) for c in chunks]
        q = [_stack_heads(act, c * C, 0, hd) for c in chunks]
        k = [_stack_heads(act, c * C, DN_WIDTH, hd) for c in chunks]
        v = [_stack_heads(act, c * C, 2 * DN_WIDTH, hd) for c in chunks]
        q = [x * lax.rsqrt(jnp.sum(x * x, axis=-1, keepdims=True) + EPS) * (hd ** -0.5) for x in q]
        yield
        k = [x * lax.rsqrt(jnp.sum(x * x, axis=-1, keepdims=True) + EPS) for x in k]
        yield
        decay = [jnp.where(causal, jnp.exp(jnp.where(causal, gc - gr, 0.0)), 0.0)
                 for gc, gr in zip(gc_col, gc_row)]
        yield
        eg = [jnp.exp(gc) for gc in gc_col]
        kb = [x * b for x, b in zip(k, beta)]
        k16 = [x.astype(BF16) for x in k]
        lower = [jnp.where(strict, _dot_nt(a.astype(BF16), b) * d, 0.0) for a, b, d in zip(kb, k16, decay)]
        yield
        t = yield from _unit_lower_inverses(lower, eye, ri, ci)
        t16 = [x.astype(BF16) for x in t]
        yield
        uw = [_dot(t, jnp.concatenate([x * b, y * e], axis=1).astype(BF16))
              for t, x, b, y, e in zip(t16, v, beta, kb, eg)]
        yield
        attn = [(_dot_nt(x.astype(BF16), y) * d).astype(BF16) for x, y, d in zip(q, k16, decay)]
        yield
        for n, c in enumerate(chunks):
            out[c] = dict(u=uw[n][:, :hd], w=uw[n][:, hd:].astype(BF16), attn=attn[n],
                          qd=(q[n] * eg[n]).astype(BF16),
                          kd=(k[n] * jnp.exp(g_last[n] - gc_col[n])).astype(BF16),
                          el=[jnp.exp(gcum[n][C - 1:C, h:h + 1]) for h in range(H)])

    def recurrence(chunks, prepared):
        for c in chunks:
            pc = prepared[c]
            r0 = c * C
            v_new = []
            q_state = []
            for h in range(H):
                rows = slice(h * C, (h + 1) * C)
                st16 = state_ref[h].astype(BF16)
                both = _dot(jnp.concatenate([pc["w"][rows], pc["qd"][rows]], axis=0), st16)
                v_new.append(pc["u"][rows] - both[:C])
                q_state.append(both[C:])
            yield
            vn16 = jnp.concatenate(v_new, axis=0).astype(BF16)
            intra = _dot(pc["attn"], vn16)
            for h in range(H):
                rows = slice(h * C, (h + 1) * C)
                state_ref[h] = state_ref[h] * pc["el"][h] + _dot_tn(pc["kd"][rows], vn16[rows])
            yield
            for h in range(H):
                rows = slice(h * C, (h + 1) * C)
                o = q_state[h] + intra[rows]
                y = _rms_scale(o) * dng * zg[r0:r0 + C, h * hd:(h + 1) * hd]
                ya_ref[r0:r0 + C, h * hd:(h + 1) * hd] = y.astype(ya_ref.dtype)
            yield

    n_chunks = ts // C
    groups = [list(range(g0, min(g0 + DN_GROUP, n_chunks))) for g0 in range(0, n_chunks, DN_GROUP)]
    prepared = {}
    pending = iter(())
    for grp in groups:
        n = 0
        for _ in chunk_prep(grp, prepared):
            n += 1
            if n % DN_WEAVE == 0:
                next(pending, None)
        for _ in pending:
            pass
        pending = recurrence(grp, prepared)
    for _ in pending:
        pass


def _deltanet(qkv, z, ab, conv_w, a_log, dt_bias, dn_norm_g, batch, seq, ts):
    N = qkv.shape[0]
    per_seq = seq // ts
    n_qkv = qkv.shape[1]
    pad = LANES - DN_HEADS
    alog_row = jnp.pad(a_log, (0, pad)).reshape(1, LANES)
    dtb_row = jnp.pad(dt_bias, (0, pad)).reshape(1, LANES)

    def row(b, s):
        return (b * per_seq + s, 0)

    def prev(b, s):
        return (jnp.maximum((b * per_seq + s) * (ts // SUBLANES) - 1, 0), 0)

    def const(b, s):
        return (0, 0)

    return pl.pallas_call(
        _deltanet_kernel,
        out_shape=jax.ShapeDtypeStruct((N, DN_WIDTH), BF16),
        grid=(batch, per_seq),
        in_specs=[pl.BlockSpec((ts, n_qkv), row),
                  pl.BlockSpec((SUBLANES, n_qkv), prev),
                  pl.BlockSpec((ts, DN_WIDTH), row),
                  pl.BlockSpec((ts, LANES), row),
                  pl.BlockSpec(conv_w.shape, const),
                  pl.BlockSpec((1, LANES), const),
                  pl.BlockSpec((1, LANES), const),
                  pl.BlockSpec((1, DN_HEAD_DIM), const)],
        out_specs=pl.BlockSpec((ts, DN_WIDTH), row),
        scratch_shapes=[pltpu.VMEM((DN_HEADS, DN_HEAD_DIM, DN_HEAD_DIM), F32)],
        compiler_params=pltpu.CompilerParams(dimension_semantics=("arbitrary", "arbitrary"),
                                             vmem_limit_bytes=VMEM_LIMIT),
        name="deltanet",
    )(qkv, qkv, z, ab, conv_w, alog_row, dtb_row, dn_norm_g.reshape(1, DN_HEAD_DIM))


def _mix_kernel(ya_ref, pu_ref, pprev_ref, ga_ref, gb_ref, x_ref, gt1_ref, sc2_ref, sh2_ref, g2_ref,
                poolw_ref, pscale_ref, wla_ref, wlb_ref, wout_ref, wr_ref, br_ref,
                h_ref, xn_ref, route_ref, counts_ref, carry_ref, *, per_seq):
    i = pl.program_id(0)
    tm = x_ref.shape[0]
    first = (i % per_seq) == 0

    @pl.when(i == 0)
    def _():
        carry_ref[...] = jnp.zeros_like(carry_ref)

    pu = pu_ref[...]
    hist = jnp.where(first, 0.0, pprev_ref[...])
    pf = jnp.concatenate([hist, pu], axis=0)
    pos = (i % per_seq) * tm + lax.broadcasted_iota(jnp.int32, (tm, 1), 0)
    pscale = pscale_ref[...]
    yb_parts = []
    for gi, win in enumerate(POOL_WINDOWS):
        lo = gi * POOL_GROUP_DIM
        xg = pf[:, lo:lo + POOL_GROUP_DIM]
        acc = xg
        span = 1
        while span < win:
            acc = acc + pltpu.roll(acc, span, 0)
            span *= 2
        cnt = jnp.minimum(pos + 1, win).astype(F32)
        pooled = acc[POOL_HIST:] / cnt - pu[:, lo:lo + POOL_GROUP_DIM]
        yg = _dot(pooled.astype(BF16), poolw_ref[gi])
        yb_parts.append(yg * pscale[:, lo:lo + POOL_GROUP_DIM])
    yb = jnp.concatenate(yb_parts, axis=1)

    lift_a = _dot(ya_ref[...], wla_ref[...])
    lift_b = _dot(yb.astype(BF16), wlb_ref[...])
    mixed = _sigmoid(ga_ref[...]) * lift_a + _sigmoid(gb_ref[...]) * lift_b
    h = x_ref[...] + gt1_ref[0] * _dot(mixed.astype(BF16), wout_ref[...])
    h_ref[...] = h
    xn = _rms_scale(h) * g2_ref[...] * (1.0 + sc2_ref[0]) + sh2_ref[0]
    xn_ref[...] = xn

    xn_hi = xn.astype(BF16)
    xn_lo = (xn - xn_hi.astype(F32)).astype(BF16)
    wr = wr_ref[...]
    wr_hi = wr.astype(BF16)
    wr_lo = (wr - wr_hi.astype(F32)).astype(BF16)
    logits = _dot(jnp.concatenate([xn_hi, xn_lo, xn_hi], axis=1),
                  jnp.concatenate([wr_hi, wr_hi, wr_lo], axis=0)) + br_ref[...]
    li = lax.broadcasted_iota(jnp.int32, (tm, LANES), 1).astype(F32)
    neg = -jnp.inf
    big = float(LANES)
    is_g = li < MOE_GROUPS
    lg = jnp.where(is_g, logits, neg)
    mg = jnp.max(lg, axis=-1, keepdims=True)
    p_grp = 1.0 / jnp.sum(jnp.where(is_g, jnp.exp(lg - mg), 0.0), axis=-1, keepdims=True)
    grp = jnp.min(jnp.where(lg == mg, li, big), axis=-1, keepdims=True)
    e_lo = MOE_GROUPS + MOE_EXPERTS_PER_GROUP * grp
    is_e = (li >= e_lo) & (li < e_lo + MOE_EXPERTS_PER_GROUP)
    le = jnp.where(is_e, logits, neg)
    m1 = jnp.max(le, axis=-1, keepdims=True)
    i1 = jnp.min(jnp.where(le == m1, li, big), axis=-1, keepdims=True)
    le2 = jnp.where(li == i1, neg, le)
    m2 = jnp.max(le2, axis=-1, keepdims=True)
    i2 = jnp.min(jnp.where(le2 == m2, li, big), axis=-1, keepdims=True)
    e2 = jnp.exp(m2 - m1)
    w1 = p_grp / (1.0 + e2)
    w2 = p_grp * e2 / (1.0 + e2)
    eid1 = i1 - MOE_GROUPS
    eid2 = i2 - MOE_GROUPS

    oh1 = li == eid1
    oh2 = li == eid2
    onehot = jnp.where(oh1 | oh2, 1.0, 0.0)
    ri = lax.broadcasted_iota(jnp.int32, (tm, tm), 0)
    ci = lax.broadcasted_iota(jnp.int32, (tm, tm), 1)
    below = jnp.where(ri > ci, 1.0, 0.0).astype(BF16)
    carry = carry_ref[0:1, :]
    before = _dot(below, onehot.astype(BF16)) + carry
    rank1 = jnp.sum(jnp.where(oh1, before, 0.0), axis=-1, keepdims=True)
    rank2 = jnp.sum(jnp.where(oh2, before, 0.0), axis=-1, keepdims=True)
    new_carry = carry + jnp.sum(onehot, axis=0, keepdims=True)
    carry_ref[...] = jnp.broadcast_to(new_carry, carry_ref.shape)
    counts_ref[...] = jnp.broadcast_to(new_carry, counts_ref.shape)

    route = jnp.where(li == 0.0, w1, 0.0)
    for lane, val in ((1, w2), (2, eid1), (3, eid2), (4, rank1), (5, rank2)):
        route = jnp.where(li == float(lane), val, route)
    route_ref[...] = route


def _mix(ya, pu, ga, gb, x2, gt1, sc2, sh2, g2, pool_w, pool_scale, w_lift_a, w_lift_b, w_out,
         w_router, b_router, seq, tm):
    N, D = x2.shape
    per_seq = seq // tm

    def row(i):
        return (i, 0)

    def prev(i):
        return (jnp.maximum(i * (tm // POOL_HIST) - 1, 0), 0)

    def mod(i):
        return (i // per_seq, 0, 0)

    def const(i):
        return (0, 0)

    return pl.pallas_call(
        functools.partial(_mix_kernel, per_seq=per_seq),
        out_shape=[jax.ShapeDtypeStruct((N, D), F32),
                   jax.ShapeDtypeStruct((N, D), F32),
                   jax.ShapeDtypeStruct((N, LANES), F32),
                   jax.ShapeDtypeStruct((SUBLANES, LANES), F32)],
        grid=(N // tm,),
        in_specs=[pl.BlockSpec((tm, DN_WIDTH), row),
                  pl.BlockSpec((tm, POOL_WIDTH), row),
                  pl.BlockSpec((POOL_HIST, POOL_WIDTH), prev),
                  pl.BlockSpec((tm, D), row),
                  pl.BlockSpec((tm, D), row),
                  pl.BlockSpec((tm, D), row),
                  pl.BlockSpec((1, 1, D), mod),
                  pl.BlockSpec((1, 1, D), mod),
                  pl.BlockSpec((1, 1, D), mod),
                  pl.BlockSpec((1, D), const),
                  pl.BlockSpec(pool_w.shape, lambda i: (0, 0, 0)),
                  pl.BlockSpec((1, POOL_WIDTH), const),
                  pl.BlockSpec(w_lift_a.shape, const),
                  pl.BlockSpec(w_lift_b.shape, const),
                  pl.BlockSpec(w_out.shape, const),
                  pl.BlockSpec(w_router.shape, const),
                  pl.BlockSpec((1, LANES), const)],
        out_specs=[pl.BlockSpec((tm, D), row),
                   pl.BlockSpec((tm, D), row),
                   pl.BlockSpec((tm, LANES), row),
                   pl.BlockSpec((SUBLANES, LANES), const)],
        scratch_shapes=[pltpu.VMEM((SUBLANES, LANES), F32)],
        compiler_params=pltpu.CompilerParams(dimension_semantics=("arbitrary",),
                                             vmem_limit_bytes=VMEM_LIMIT),
        name="mix",
    )(ya, pu, pu, ga, gb, x2, gt1, sc2, sh2, g2, pool_w, pool_scale, w_lift_a, w_lift_b, w_out,
      w_router, b_router)


def _dest_kernel(route_ref, pstart_ref, dest_ref):
    route = route_ref[...]
    tm = route.shape[0]
    li = lax.broadcasted_iota(jnp.int32, (tm, LANES), 1).astype(F32)
    pstart = pstart_ref[...]
    out = jnp.zeros((tm, LANES), F32)
    for k in range(2):
        eid = route[:, 2 + k:3 + k]
        rank = route[:, 4 + k:5 + k]
        slot = jnp.sum(jnp.where(li == eid, pstart, 0.0), axis=-1, keepdims=True) + rank
        out = jnp.where(li == float(k), slot, out)
    dest_ref[...] = out.astype(jnp.int32)


def _dest(route, pstart_row, tm):
    N = route.shape[0]
    return pl.pallas_call(
        _dest_kernel,
        out_shape=jax.ShapeDtypeStruct((N, LANES), jnp.int32),
        grid=(N // tm,),
        in_specs=[pl.BlockSpec((tm, LANES), lambda i: (i, 0)),
                  pl.BlockSpec((1, LANES), lambda i: (0, 0))],
        out_specs=pl.BlockSpec((tm, LANES), lambda i: (i, 0)),
        compiler_params=pltpu.CompilerParams(dimension_semantics=("arbitrary",)),
        name="dest",
    )(route, pstart_row)


def _dispatch_kernel(pend_ref, padded_ref, nused_ref, d0_ref, d1_ref, x_ref, xs_ref, zero_ref, sem, zsem):
    tm = x_ref.shape[0]
    nb = xs_ref.shape[0] // MOE_ROWS

    @pl.when(pl.program_id(0) == 0)
    def _():
        zero_ref[...] = jnp.zeros_like(zero_ref)

        def zero_block(row0):
            return pltpu.make_async_copy(zero_ref, xs_ref.at[pl.ds(row0, MOE_ROWS)], zsem)

        def tail_block(e):
            return zero_block(pl.multiple_of(pend_ref[e] - MOE_ROWS, MOE_ROWS))

        def unused_block(j):
            return zero_block(pl.multiple_of(j * MOE_ROWS, MOE_ROWS))

        def for_each_zero_block(act):
            def per_expert(e, carry):
                @pl.when(padded_ref[e] > 0)
                def _():
                    act(tail_block(e))
                return carry

            def per_unused(j, carry):
                act(unused_block(j))
                return carry

            lax.fori_loop(0, MOE_EXPERTS, per_expert, 0)
            lax.fori_loop(nused_ref[0], nb, per_unused, 0)

        for_each_zero_block(lambda cp: cp.start())
        for_each_zero_block(lambda cp: cp.wait())

    def row_copy(t, d):
        return pltpu.make_async_copy(x_ref.at[pl.ds(t, 1)], xs_ref.at[pl.ds(d, 1)], sem)

    def issue(t, carry):
        row_copy(t, d0_ref[0, t]).start(priority=0)
        row_copy(t, d1_ref[0, t]).start(priority=1)
        return carry

    lax.fori_loop(0, tm, issue, 0, unroll=8)

    def drain(t, carry):
        row_copy(t, d0_ref[0, t]).wait()
        row_copy(t, d1_ref[0, t]).wait()
        return carry

    lax.fori_loop(0, tm, drain, 0, unroll=8)


def _dispatch(pend, padded, nused, d0, d1, xn, n_slots, tm):
    N, D = xn.shape
    nt = N // tm
    smem = functools.partial(pl.BlockSpec, (None, 1, tm), lambda i, *_: (i, 0, 0),
                             memory_space=pltpu.SMEM)
    return pl.pallas_call(
        _dispatch_kernel,
        out_shape=jax.ShapeDtypeStruct((n_slots, D), xn.dtype),
        grid_spec=pltpu.PrefetchScalarGridSpec(
            num_scalar_prefetch=3,
            grid=(nt,),
            in_specs=[smem(), smem(), pl.BlockSpec((tm, D), lambda i, *_: (i, 0))],
            out_specs=pl.BlockSpec(memory_space=pl.ANY),
            scratch_shapes=[pltpu.VMEM((MOE_ROWS, D), xn.dtype), pltpu.SemaphoreType.DMA,
                            pltpu.SemaphoreType.DMA]),
        compiler_params=pltpu.CompilerParams(dimension_semantics=("arbitrary",),
                                             vmem_limit_bytes=VMEM_LIMIT),
        name="dispatch",
    )(pend, padded, nused, d0.reshape(nt, 1, tm), d1.reshape(nt, 1, tm), xn)


def _experts_kernel(blk_e_ref, nused_ref, x_ref, wg_ref, wu_ref, wd_ref, y_ref):
    del blk_e_ref
    used = pl.program_id(0) < nused_ref[0]

    @pl.when(used)
    def _():
        x = x_ref[...].astype(BF16)
        hid = _silu(_dot(x, wg_ref[...].astype(BF16))) * _dot(x, wu_ref[...].astype(BF16))
        y_ref[...] = _dot(hid.astype(BF16), wd_ref[...].astype(BF16))

    @pl.when(jnp.logical_not(used))
    def _():
        y_ref[...] = jnp.zeros_like(y_ref)


def _experts(blk_e, nused, xs, w_gate, w_up, w_down):
    P, D = xs.shape
    nb = P // MOE_ROWS
    dff = w_gate.shape[2]

    def rows(j, blk_e_ref, nused_ref):
        return (jnp.maximum(jnp.minimum(j, nused_ref[0] - 1), 0), 0)

    def wsel(j, blk_e_ref, nused_ref):
        return (blk_e_ref[j], 0, 0)

    return pl.pallas_call(
        _experts_kernel,
        out_shape=jax.ShapeDtypeStruct((P, D), F32),
        grid_spec=pltpu.PrefetchScalarGridSpec(
            num_scalar_prefetch=2,
            grid=(nb,),
            in_specs=[pl.BlockSpec((MOE_ROWS, D), rows),
                      pl.BlockSpec((None, D, dff), wsel),
                      pl.BlockSpec((None, D, dff), wsel),
                      pl.BlockSpec((None, dff, D), wsel)],
            out_specs=pl.BlockSpec((MOE_ROWS, D), lambda j, blk_e_ref, nused_ref: (j, 0))),
        compiler_params=pltpu.CompilerParams(dimension_semantics=("arbitrary",),
                                             vmem_limit_bytes=VMEM_LIMIT),
        name="experts",
    )(blk_e, nused, xs, w_gate, w_up, w_down)


def _combine_kernel(d0_ref, d1_ref, d0n_ref, d1n_ref, route_ref, h_ref, gt2_ref, gf_ref, y_ref, o_ref,
                    buf0_ref, buf1_ref, sems):
    i = pl.program_id(0)
    tm = h_ref.shape[0]
    slot = i % 2

    def row_copy(buf_ref, sl, t, d):
        return pltpu.make_async_copy(y_ref.at[pl.ds(d, 1)], buf_ref.at[sl, pl.ds(t, 1)], sems.at[sl])

    def gather(da_ref, db_ref, sl):
        def issue(t, carry):
            row_copy(buf0_ref, sl, t, da_ref[0, t]).start(priority=0)
            row_copy(buf1_ref, sl, t, db_ref[0, t]).start(priority=1)
            return carry

        lax.fori_loop(0, tm, issue, 0, unroll=8)

    @pl.when(i == 0)
    def _():
        gather(d0_ref, d1_ref, slot)

    @pl.when(i + 1 < pl.num_programs(0))
    def _():
        gather(d0n_ref, d1n_ref, 1 - slot)

    def drain(t, carry):
        row_copy(buf0_ref, slot, t, d0_ref[0, t]).wait()
        row_copy(buf1_ref, slot, t, d1_ref[0, t]).wait()
        return carry

    lax.fori_loop(0, tm, drain, 0, unroll=8)

    route = route_ref[...]
    moe = buf0_ref[slot] * route[:, 0:1] + buf1_ref[slot] * route[:, 1:2]
    h = h_ref[...] + gt2_ref[0] * moe
    o_ref[...] = _rms_scale(h) * gf_ref[...]


def _combine(d0, d1, route, h, gt2, gf, y, seq, tm):
    N, D = h.shape
    nt = N // tm
    per_seq = seq // tm
    cur = functools.partial(pl.BlockSpec, (None, 1, tm), lambda i: (i, 0, 0), memory_space=pltpu.SMEM)
    nxt = functools.partial(pl.BlockSpec, (None, 1, tm), lambda i: (jnp.minimum(i + 1, nt - 1), 0, 0),
                            memory_space=pltpu.SMEM)
    d0 = d0.reshape(nt, 1, tm)
    d1 = d1.reshape(nt, 1, tm)
    return pl.pallas_call(
        _combine_kernel,
        out_shape=jax.ShapeDtypeStruct((N, D), F32),
        grid=(nt,),
        in_specs=[cur(), cur(), nxt(), nxt(),
                  pl.BlockSpec((tm, LANES), lambda i: (i, 0)),
                  pl.BlockSpec((tm, D), lambda i: (i, 0)),
                  pl.BlockSpec((1, 1, D), lambda i: (i // per_seq, 0, 0)),
                  pl.BlockSpec((1, D), lambda i: (0, 0)),
                  pl.BlockSpec(memory_space=pl.ANY)],
        out_specs=pl.BlockSpec((tm, D), lambda i: (i, 0)),
        scratch_shapes=[pltpu.VMEM((2, tm, D), F32), pltpu.VMEM((2, tm, D), F32),
                        pltpu.SemaphoreType.DMA((2,))],
        compiler_params=pltpu.CompilerParams(dimension_semantics=("arbitrary",),
                                             vmem_limit_bytes=VMEM_LIMIT),
        name="combine",
    )(d0, d1, d0, d1, route, h, gt2, gf, y)


def _layer(h2, mod, batch, seq, norm1_g, w_in, conv_w, a_log, dt_bias, dn_norm_g, pool_w, pool_scale,
           w_lift_a, w_lift_b, w_out, norm2_g, w_rg, b_rg, w_re, b_re, w_gate, w_up, w_down, final_g):
    N, D = h2.shape
    sh1, sc1, gt1, sh2, sc2, gt2 = [m.reshape(batch, 1, D) for m in jnp.split(mod, 6, axis=-1)]

    o_z = 3 * DN_WIDTH
    o_a = o_z + DN_WIDTH
    o_pu = o_a + 2 * DN_HEADS
    o_g = o_pu + POOL_WIDTH
    w_qkvz = w_in[:, :o_a].astype(BF16)
    w_ab = jnp.pad(w_in[:, o_a:o_pu], ((0, 0), (0, LANES - 2 * DN_HEADS))).astype(BF16)
    w_pu = w_in[:, o_pu:o_g].astype(BF16)
    w_g = w_in[:, o_g:].astype(BF16)

    tm = min(256, seq)
    qkv, z, ab, pu, ga, gb = _inproj(h2, norm1_g.reshape(1, D), sc1, sh1, w_qkvz, w_ab, w_pu, w_g, seq,
                                     min(512, seq))

    ts = min(512, seq)
    ya = _deltanet(qkv, z, ab, conv_w, a_log, dt_bias, dn_norm_g, batch, seq, ts)

    n_r = MOE_GROUPS + MOE_EXPERTS
    w_router = jnp.pad(jnp.concatenate([w_rg, w_re], axis=1), ((0, 0), (0, LANES - n_r)))
    b_router = jnp.pad(jnp.concatenate([b_rg, b_re]), (0, LANES - n_r)).reshape(1, LANES)
    h_mid, xn2, route, counts = _mix(
        ya, pu, ga, gb, h2, gt1, sc2, sh2, norm2_g.reshape(1, D), pool_w.astype(BF16),
        pool_scale.reshape(1, POOL_WIDTH), w_lift_a.astype(BF16), w_lift_b.astype(BF16),
        w_out.astype(BF16), w_router, b_router, seq, tm)

    cnt = counts[0, :MOE_EXPERTS].astype(jnp.int32)
    padded = (cnt + MOE_ROWS - 1) // MOE_ROWS * MOE_ROWS
    pend = jnp.cumsum(padded)
    pstart = pend - padded
    n_slots = 2 * N + MOE_EXPERTS * MOE_ROWS
    nb = n_slots // MOE_ROWS
    nused = (pend[-1] // MOE_ROWS).astype(jnp.int32).reshape(1)
    blk_row0 = jnp.arange(nb, dtype=jnp.int32) * MOE_ROWS
    blk_e = jnp.minimum(jnp.sum((pend[None, :] <= blk_row0[:, None]).astype(jnp.int32), axis=1),
                        MOE_EXPERTS - 1)
    pstart_row = jnp.pad(pstart.astype(F32), (0, LANES - MOE_EXPERTS)).reshape(1, LANES)

    dest = _dest(route, pstart_row, min(2048, N))
    d0 = dest[:, 0]
    d1 = dest[:, 1]
    xs = _dispatch(pend.astype(jnp.int32), padded, nused, d0, d1, xn2, n_slots, min(1024, N))
    y = _experts(blk_e, nused, xs, w_gate, w_up, w_down)
    return _combine(d0, d1, route, h_mid, gt2, final_g.reshape(1, D), y, seq, min(512, seq))


def kernel(x, c, w_ada, b_ada, norm1_g, w_in, conv_w, a_log, dt_bias, dn_norm_g, pool_w, pool_scale,
           w_lift_a, w_lift_b, w_out, norm2_g, w_router_group, b_router_group, w_router_expert,
           b_router_expert, w_gate, w_up, w_down, final_norm_g):
    batch, seq, D = x.shape
    depth = w_ada.shape[0]
    assert depth == 1, "the combine kernel fuses the final rmsnorm, so exactly one layer is supported"
    assert seq % DN_CHUNK == 0 and D % LANES == 0
    h2 = x.reshape(batch * seq, D)
    l = 0
    mod = _ada(c, w_ada[l], b_ada[l])
    out = _layer(h2, mod, batch, seq, norm1_g[l], w_in[l], conv_w[l], a_log[l], dt_bias[l], dn_norm_g[l],
                 pool_w[l], pool_scale[l], w_lift_a[l], w_lift_b[l], w_out[l], norm2_g[l],
                 w_router_group[l], b_router_group[l], w_router_expert[l], b_router_expert[l],
                 w_gate[l], w_up[l], w_down[l], final_norm_g)
    return out.reshape(batch, seq, D)
```
